```python
import functools
import jax, jax.numpy as jnp
from jax import lax
import numpy as np

D_MODEL = 4096
BATCH = 1
SEQ = 8192
DEPTH = 1
DEC_BATCH = 128
DEC_SEQ = 1
PAST_LEN = 2048
PAGE_SIZE = 128

RW_HEADS = 32
RW_HEAD_DIM = 64
RW_WIDTH = RW_HEADS * RW_HEAD_DIM
DECAY_LORA = 96
AAA_LORA = 96
GATE_LORA = 256
RW_GN_EPS = 64e-5
RW_SPLITS = [RW_WIDTH, 2 * RW_WIDTH, 3 * RW_WIDTH, 3 * RW_WIDTH + DECAY_LORA, 3 * RW_WIDTH + DECAY_LORA + AAA_LORA]
RW_IN = 3 * RW_WIDTH + DECAY_LORA + AAA_LORA + GATE_LORA
AT_HEADS = 16
AT_HEAD_DIM = 128
AT_WIDTH = AT_HEADS * AT_HEAD_DIM
IDX_HEADS = 32
IDX_DIM = 128
IDX_SCALE = (IDX_HEADS * IDX_DIM) ** -0.5
TOPK_MAX = 256
Q_BLOCK = 128
AT_SPLITS = [AT_WIDTH, 2 * AT_WIDTH, 3 * AT_WIDTH, 3 * AT_WIDTH + IDX_HEADS * IDX_DIM, 3 * AT_WIDTH + IDX_HEADS * IDX_DIM + IDX_DIM]
AT_IN = 3 * AT_WIDTH + IDX_HEADS * IDX_DIM + IDX_DIM + IDX_HEADS
GATE_IN = 2 * D_MODEL
IN_WIDTH = RW_IN + AT_IN + GATE_IN
PEER_HEADS = 8
PEER_NKEYS = 128
PEER_EXPERTS = PEER_NKEYS * PEER_NKEYS
PEER_QDIM = 256
PEER_TOPK = 16
TOKEN_BLOCK = 128
LN_EPS = 1e-5
ALPHA = (2 * DEPTH) ** 0.25
BETA = (8 * DEPTH) ** -0.25

kernel_name = 'rwkv7_dsa_peer_hybrid_step'


def layer_norm(x, g, b):
    xf = x.astype(jnp.float32)
    mu = jnp.mean(xf, -1, keepdims=True)
    var = jnp.mean(jnp.square(xf - mu), -1, keepdims=True)
    return ((xf - mu) * lax.rsqrt(var + LN_EPS) * g + b).astype(x.dtype)


def ada_mod(c, w_ada, b_ada):
    m = jax.nn.silu(c) @ w_ada + b_ada
    return jnp.split(m[:, None, :], 6, axis=-1)


def alibi_slopes():
    return 2.0 ** (-8.0 * jnp.arange(1, AT_HEADS + 1, dtype=jnp.float32) / AT_HEADS)


def wkv7_scan(state, r, w, k, v, a, b):
    def step(S, inp):
        r_t, w_t, k_t, v_t, a_t, b_t = inp
        sa = jnp.einsum('bhvk,bhk->bhv', S, a_t)
        S = S * w_t[:, :, None, :] + sa[..., None] * b_t[:, :, None, :] + v_t[..., None] * k_t[:, :, None, :]
        return S, jnp.einsum('bhvk,bhk->bhv', S, r_t)
    xs = tuple(jnp.moveaxis(t.astype(jnp.float32), 1, 0) for t in (r, w, k, v, a, b))
    S, ys = lax.scan(step, state.astype(jnp.float32), xs)
    return S, jnp.moveaxis(ys, 0, 1)


def rwkv_branch(z_rw, z_prev, wkv0, rw_mu, rw_w0, rw_w2, rw_a0, rw_a2, rw_g2, rw_kk, rw_ka, rw_rk, rw_gn_g, rw_gn_b):
    B, T, _ = z_rw.shape
    f32 = jnp.float32
    z_shift = jnp.concatenate([z_prev[:, None, :], z_rw[:, :-1]], axis=1)
    zm = z_rw + (z_shift - z_rw) * rw_mu
    r, k, v, zw, za, zg = jnp.split(zm, RW_SPLITS, axis=-1)
    logw = -jax.nn.softplus(-(rw_w0 + jnp.tanh(zw) @ rw_w2).astype(f32)) - 0.5
    decay = jnp.exp(-jnp.exp(logw))
    a = jax.nn.sigmoid((rw_a0 + za @ rw_a2).astype(f32))
    g = jax.nn.sigmoid(zg) @ rw_g2
    hd = lambda t: t.reshape(B, T, RW_HEADS, RW_HEAD_DIM)
    kk = hd((k * rw_kk).astype(f32))
    kk = kk / jnp.maximum(jnp.linalg.norm(kk, axis=-1, keepdims=True), 1e-12)
    k = k.astype(f32) * (1.0 + (a - 1.0) * rw_ka)
    state, y = wkv7_scan(wkv0, hd(r), hd(decay), hd(k), hd(v), -kk, kk * hd(a))
    mu = jnp.mean(y, -1, keepdims=True)
    var = jnp.mean(jnp.square(y - mu), -1, keepdims=True)
    yn = ((y - mu) * lax.rsqrt(var + RW_GN_EPS)).reshape(B, T, RW_WIDTH) * rw_gn_g + rw_gn_b
    bonus = jnp.sum(hd(r).astype(f32) * hd(k) * rw_rk, -1, keepdims=True) * hd(v).astype(f32)
    out = (yn + bonus.reshape(B, T, RW_WIDTH)) * g
    return out.astype(z_rw.dtype), state.astype(wkv0.dtype)


def indexer_topk(q_idx, w_idx, k_idx, tq, topk):
    L = k_idx.shape[1]
    s = jax.nn.relu(jnp.einsum('bqhd,bld->bqhl', q_idx, k_idx).astype(jnp.float32))
    score = jnp.einsum('bqh,bqhl->bql', w_idx.astype(jnp.float32), s) * IDX_SCALE
    causal = jnp.arange(L)[None, None, :] <= tq[:, :, None]
    _, idx = lax.top_k(jnp.where(causal, score, -jnp.inf), topk)
    return idx, idx <= tq[..., None]


def sparse_attend(q, k_sel, v_sel, idx, valid, tq, slopes):
    logits = jnp.einsum('bqhd,bqkhd->bqhk', q, k_sel).astype(jnp.float32) * AT_HEAD_DIM ** -0.5
    dist = (tq[..., None] - idx).astype(jnp.float32)[:, :, None, :]
    logits = jnp.where(valid[:, :, None, :], logits - slopes[:, None] * dist, -jnp.inf)
    p = jax.nn.softmax(logits, axis=-1)
    return jnp.einsum('bqhk,bqkhd->bqhd', p.astype(v_sel.dtype), v_sel)


def prompt_sparse_attention(q, k, v, q_idx, k_idx, w_idx, slopes):
    B, S = q.shape[:2]
    topk = min(TOPK_MAX, S // 4)
    nb = S // Q_BLOCK
    blk = lambda t: jnp.moveaxis(t.reshape((B, nb, Q_BLOCK) + t.shape[2:]), 1, 0)
    gather = jax.vmap(lambda rows, ii: rows[ii])

    def one_block(inp):
        qb, qib, wib, t0 = inp
        tq = jnp.broadcast_to(t0 + jnp.arange(Q_BLOCK), (B, Q_BLOCK))
        idx, valid = indexer_topk(qib, wib, k_idx, tq, topk)
        return sparse_attend(qb, gather(k, idx), gather(v, idx), idx, valid, tq, slopes)

    out = lax.map(one_block, (blk(q), blk(q_idx), blk(w_idx), jnp.arange(nb) * Q_BLOCK))
    return jnp.moveaxis(out, 0, 1).reshape(B, S, AT_WIDTH)


def sample_sparse_attention(q, k, v, q_idx, k_idx, w_idx, cache_k, cache_v, cache_kidx, page_table, slopes):
    DB, DS = q.shape[:2]
    n_pages = PAST_LEN // PAGE_SIZE
    L = PAST_LEN + DS
    topk = min(TOPK_MAX, L // 4)
    kidx_past = cache_kidx[page_table].reshape(DB, n_pages * PAGE_SIZE, IDX_DIM)
    kidx_all = jnp.concatenate([kidx_past, k_idx], axis=1)
    tq = PAST_LEN + jnp.broadcast_to(jnp.arange(DS), (DB, DS))
    idx, valid = indexer_topk(q_idx, w_idx, kidx_all, tq, topk)
    in_past = idx < PAST_LEN
    pidx = jnp.minimum(idx, PAST_LEN - 1)
    bsel = jnp.arange(DB)[:, None, None]
    phys = page_table[bsel, pidx // PAGE_SIZE]
    slot = pidx % PAGE_SIZE
    nidx = jnp.clip(idx - PAST_LEN, 0, DS - 1)
    pick = lambda pool, new: jnp.where(in_past[..., None, None], pool[phys, slot], new[bsel, nidx])
    out = sparse_attend(q, pick(cache_k, k), pick(cache_v, v), idx, valid, tq, slopes)
    return out.reshape(DB, DS, AT_WIDTH)


def token_mixer(h, z_prev, wkv0, attn_fn, w_in, rw_mu, rw_w0, rw_w2, rw_a0, rw_a2, rw_g2, rw_kk, rw_ka, rw_rk,
                rw_gn_g, rw_gn_b, w_up_rw, w_up_at, w_out):
    B, T, _ = h.shape
    z = h @ w_in
    z_rw, z_at, z_gate = jnp.split(z, [RW_IN, RW_IN + AT_IN], axis=-1)
    o_rw, wkv = rwkv_branch(z_rw, z_prev, wkv0, rw_mu, rw_w0, rw_w2, rw_a0, rw_a2, rw_g2, rw_kk, rw_ka, rw_rk,
                            rw_gn_g, rw_gn_b)
    q, k, v, qi, ki, wi = jnp.split(z_at, AT_SPLITS, axis=-1)
    q = q.reshape(B, T, AT_HEADS, AT_HEAD_DIM)
    k = k.reshape(B, T, AT_HEADS, AT_HEAD_DIM)
    v = v.reshape(B, T, AT_HEADS, AT_HEAD_DIM)
    qi = qi.reshape(B, T, IDX_HEADS, IDX_DIM)
    o_at = attn_fn(q, k, v, qi, ki, wi)
    g_rw, g_at = jnp.split(jax.nn.sigmoid(z_gate), 2, axis=-1)
    merged = g_rw * (o_rw @ w_up_rw) + g_at * (o_at @ w_up_at)
    return merged @ w_out, (k, v, ki, wkv, z_rw[:, -1])


def peer_ffn(h, peer_wq, peer_keys, peer_u, peer_v):
    B, T, D = h.shape
    n = B * T
    pad = (-n) % TOKEN_BLOCK
    blocks = jnp.pad(h.reshape(n, D), ((0, pad), (0, 0))).reshape(-1, TOKEN_BLOCK, D)

    def one(xb):
        q = (xb @ peer_wq).reshape(TOKEN_BLOCK, PEER_HEADS, 2, PEER_QDIM // 2)
        s = jnp.einsum('nhpd,hpkd->nhpk', q, peer_keys).astype(jnp.float32)
        sv, si = lax.top_k(s, PEER_TOPK)
        cand = sv[:, :, 0, :, None] + sv[:, :, 1, None, :]
        cs, ci = lax.top_k(cand.reshape(TOKEN_BLOCK, PEER_HEADS, PEER_TOPK * PEER_TOPK), PEER_TOPK)
        e = (jnp.take_along_axis(si[:, :, 0], ci // PEER_TOPK, -1) * PEER_NKEYS
             + jnp.take_along_axis(si[:, :, 1], ci % PEER_TOPK, -1))
        gate = jax.nn.softmax(cs, axis=-1)
        act = jax.nn.gelu(jnp.einsum('nd,nhkd->nhk', xb, peer_u[e]).astype(jnp.float32), approximate=False)
        return jnp.einsum('nhk,nhkd->nd', (gate * act).astype(xb.dtype), peer_v[e])

    out = lax.map(one, blocks).reshape(-1, D)[:n]
    return out.reshape(B, T, D)


def setup_inputs(seed: int = 0) -> dict:
    key = jax.random.key(seed)
    ks = iter(jax.random.split(key, 48))
    nrm = lambda shape, scale: jax.random.normal(next(ks), shape, jnp.float32) * scale
    n_pages = PAST_LEN // PAGE_SIZE
    used = DEC_BATCH * n_pages
    n_phys = used + max(1, used // 4)
    perm = jax.random.permutation(next(ks), n_phys)
    page_table = perm[:used].reshape(DEC_BATCH, n_pages).astype(jnp.int32)
    Ld = (DEPTH,)
    return {
        'x_prompt': nrm((BATCH, SEQ, D_MODEL), 1.0),
        'x_sample': nrm((DEC_BATCH, DEC_SEQ, D_MODEL), 1.0),
        'c_prompt': nrm((BATCH, D_MODEL), 1.0),
        'c_sample': nrm((DEC_BATCH, D_MODEL), 1.0),
        'cache_k': nrm(Ld + (n_phys, PAGE_SIZE, AT_HEADS, AT_HEAD_DIM), 1.0),
        'cache_v': nrm(Ld + (n_phys, PAGE_SIZE, AT_HEADS, AT_HEAD_DIM), 1.0),
        'cache_kidx': nrm(Ld + (n_phys, PAGE_SIZE, IDX_DIM), 1.0),
        'state_wkv': nrm(Ld + (DEC_BATCH, RW_HEADS, RW_HEAD_DIM, RW_HEAD_DIM), 0.5),
        'state_shift': nrm(Ld + (DEC_BATCH, RW_IN), 1.0),
        'page_table': page_table,
        'w_ada': nrm(Ld + (D_MODEL, 6 * D_MODEL), 0.5 * D_MODEL ** -0.5),
        'b_ada': nrm(Ld + (6 * D_MODEL,), 0.02),
        'w_in': nrm(Ld + (D_MODEL, IN_WIDTH), D_MODEL ** -0.5),
        'rw_mu': 0.5 + nrm(Ld + (RW_IN,), 0.1),
        'rw_w0': nrm(Ld + (RW_WIDTH,), 1.0) - 0.5,
        'rw_w2': nrm(Ld + (DECAY_LORA, RW_WIDTH), 0.5 * DECAY_LORA ** -0.5),
        'rw_a0': nrm(Ld + (RW_WIDTH,), 0.5),
        'rw_a2': nrm(Ld + (AAA_LORA, RW_WIDTH), 0.5 * AAA_LORA ** -0.5),
        'rw_g2': nrm(Ld + (GATE_LORA, RW_WIDTH), GATE_LORA ** -0.5),
        'rw_kk': 0.85 + nrm(Ld + (RW_WIDTH,), 0.05),
        'rw_ka': 1.0 + nrm(Ld + (RW_WIDTH,), 0.05),
        'rw_rk': nrm(Ld + (RW_HEADS, RW_HEAD_DIM), 0.1),
        'rw_gn_g': 1.0 + nrm(Ld + (RW_WIDTH,), 0.02),
        'rw_gn_b': nrm(Ld + (RW_WIDTH,), 0.02),
        'w_up_rw': nrm(Ld + (RW_WIDTH, D_MODEL), RW_WIDTH ** -0.5),
        'w_up_at': nrm(Ld + (AT_WIDTH, D_MODEL), AT_WIDTH ** -0.5),
        'w_out': nrm(Ld + (D_MODEL, D_MODEL), BETA * D_MODEL ** -0.5),
        'ln1_g': 1.0 + nrm(Ld + (D_MODEL,), 0.02),
        'ln1_b': nrm(Ld + (D_MODEL,), 0.02),
        'peer_wq': nrm(Ld + (D_MODEL, PEER_HEADS * PEER_QDIM), D_MODEL ** -0.5),
        'peer_keys': nrm(Ld + (PEER_HEADS, 2, PEER_NKEYS, PEER_QDIM // 2), (PEER_QDIM // 2) ** -0.5),
        'peer_u': nrm(Ld + (PEER_EXPERTS, D_MODEL), D_MODEL ** -0.5),
        'peer_v': nrm(Ld + (PEER_EXPERTS, D_MODEL), BETA),
        'ln2_g': 1.0 + nrm(Ld + (D_MODEL,), 0.02),
        'ln2_b': nrm(Ld + (D_MODEL,), 0.02),
    }


def reference(x_prompt, x_sample, c_prompt, c_sample, cache_k, cache_v, cache_kidx, state_wkv, state_shift, page_table,
              w_ada, b_ada, w_in, rw_mu, rw_w0, rw_w2, rw_a0, rw_a2, rw_g2, rw_kk, rw_ka, rw_rk, rw_gn_g, rw_gn_b,
              w_up_rw, w_up_at, w_out, ln1_g, ln1_b, peer_wq, peer_keys, peer_u, peer_v, ln2_g, ln2_b):
    slopes = alibi_slopes()
    yp, ys = x_prompt, x_sample
    Bp = x_prompt.shape[0]
    zero_shift = jnp.zeros((Bp, RW_IN), x_prompt.dtype)
    zero_wkv = jnp.zeros((Bp, RW_HEADS, RW_HEAD_DIM, RW_HEAD_DIM), x_prompt.dtype)
    new_p = [[], [], [], [], []]
    new_s = [[], [], [], [], []]
    for l in range(DEPTH):
        mp = ada_mod(c_prompt, w_ada[l], b_ada[l])
        ms = ada_mod(c_sample, w_ada[l], b_ada[l])
        mix_w = (w_in[l], rw_mu[l], rw_w0[l], rw_w2[l], rw_a0[l], rw_a2[l], rw_g2[l], rw_kk[l], rw_ka[l], rw_rk[l],
                 rw_gn_g[l], rw_gn_b[l], w_up_rw[l], w_up_at[l], w_out[l])
        prompt_attn = functools.partial(prompt_sparse_attention, slopes=slopes)
        sample_attn = functools.partial(sample_sparse_attention, cache_k=cache_k[l], cache_v=cache_v[l],
                                        cache_kidx=cache_kidx[l], page_table=page_table, slopes=slopes)
        o_p, st_p = token_mixer(yp * (1 + mp[1]) + mp[0], zero_shift, zero_wkv, prompt_attn, *mix_w)
        o_s, st_s = token_mixer(ys * (1 + ms[1]) + ms[0], state_shift[l], state_wkv[l], sample_attn, *mix_w)
        yp = layer_norm(ALPHA * yp + mp[2] * o_p, ln1_g[l], ln1_b[l])
        ys = layer_norm(ALPHA * ys + ms[2] * o_s, ln1_g[l], ln1_b[l])
        for i in range(5):
            new_p[i].append(st_p[i])
            new_s[i].append(st_s[i])
        peer_w = (peer_wq[l], peer_keys[l], peer_u[l], peer_v[l])
        yp = layer_norm(ALPHA * yp + mp[5] * peer_ffn(yp * (1 + mp[4]) + mp[3], *peer_w), ln2_g[l], ln2_b[l])
        ys = layer_norm(ALPHA * ys + ms[5] * peer_ffn(ys * (1 + ms[4]) + ms[3], *peer_w), ln2_g[l], ln2_b[l])
    k_p, v_p, kidx_p, wkv_p, shift_p = [jnp.stack(t) for t in new_p]
    k_s, v_s, kidx_s, wkv_s, shift_s = [jnp.stack(t) for t in new_s]
    return (yp, ys, k_p, v_p, kidx_p, wkv_p, shift_p, k_s, v_s, kidx_s, wkv_s, shift_s)
```

```python
import functools

import jax
import jax.numpy as jnp
from jax import lax
from jax.experimental import pallas as pl
from jax.experimental.pallas import tpu as pltpu

PAGE_SIZE = 128
TOPK_MAX = 256
Q_BLOCK = 128
PEER_TOPK = 16
TOKEN_BLOCK = 128
RW_GN_EPS = 64e-5
LN_EPS = 1e-5

VMEM_LIMIT_BYTES = 56 * 1024 * 1024


def _round_up(n, m):
    return (n + m - 1) // m * m


def _matmul_body(a_ref, b_ref, o_ref):
    o_ref[...] = jnp.dot(a_ref[...], b_ref[...], preferred_element_type=jnp.float32)


def _matmul(a, b, tm=512, tn=512):
    m, k = a.shape
    _, n = b.shape
    tm = min(tm, _round_up(m, 8))
    mp, np_ = _round_up(m, tm), _round_up(n, tn)
    a = a.astype(jnp.bfloat16)
    b = b.astype(jnp.bfloat16)
    if mp != m:
        a = jnp.pad(a, ((0, mp - m), (0, 0)))
    if np_ != n:
        b = jnp.pad(b, ((0, 0), (0, np_ - n)))
    out = pl.pallas_call(
        _matmul_body,
        grid=(mp // tm, np_ // tn),
        in_specs=[
            pl.BlockSpec((tm, k), lambda i, j: (i, 0)),
            pl.BlockSpec((k, tn), lambda i, j: (0, j)),
        ],
        out_specs=pl.BlockSpec((tm, tn), lambda i, j: (i, j)),
        out_shape=jax.ShapeDtypeStruct((mp, np_), jnp.float32),
        compiler_params=pltpu.CompilerParams(
            dimension_semantics=("arbitrary", "arbitrary"),
            vmem_limit_bytes=VMEM_LIMIT_BYTES,
        ),
        name="matmul",
    )(a, b)
    return out[:m, :n]


def _mm3(x, w):
    b, t, d = x.shape
    return _matmul(x.reshape(b * t, d), w).reshape(b, t, -1)


def _layer_norm(x, g, b):
    mu = jnp.mean(x, -1, keepdims=True)
    var = jnp.mean(jnp.square(x - mu), -1, keepdims=True)
    return (x - mu) * lax.rsqrt(var + LN_EPS) * g + b


def _ada_mod(c, w_ada, b_ada):
    m = _matmul(jax.nn.silu(c), w_ada, tn=1024) + b_ada
    return jnp.split(m[:, None, :], 6, axis=-1)


def _wkv7_scan(state, r, w, k, v, a, b):
    def step(S, inp):
        r_t, w_t, k_t, v_t, a_t, b_t = inp
        sa = jnp.einsum('bhvk,bhk->bhv', S, a_t)
        S = S * w_t[:, :, None, :] + sa[..., None] * b_t[:, :, None, :] + v_t[..., None] * k_t[:, :, None, :]
        return S, jnp.einsum('bhvk,bhk->bhv', S, r_t)
    xs = tuple(jnp.moveaxis(t, 1, 0) for t in (r, w, k, v, a, b))
    S, ys = lax.scan(step, state, xs)
    return S, jnp.moveaxis(ys, 0, 1)


def _rwkv_branch(z_rw, z_prev, wkv0, rw_mu, rw_w0, rw_w2, rw_a0, rw_a2, rw_g2, rw_kk, rw_ka, rw_rk, rw_gn_g, rw_gn_b):
    B, T, _ = z_rw.shape
    H, N = rw_rk.shape
    W = H * N
    d_lora, a_lora = rw_w2.shape[0], rw_a2.shape[0]
    splits = [W, 2 * W, 3 * W, 3 * W + d_lora, 3 * W + d_lora + a_lora]
    z_shift = jnp.concatenate([z_prev[:, None, :], z_rw[:, :-1]], axis=1)
    zm = z_rw + (z_shift - z_rw) * rw_mu
    r, k, v, zw, za, zg = jnp.split(zm, splits, axis=-1)
    logw = -jax.nn.softplus(-(rw_w0 + jnp.tanh(zw) @ rw_w2)) - 0.5
    decay = jnp.exp(-jnp.exp(logw))
    a = jax.nn.sigmoid(rw_a0 + za @ rw_a2)
    g = jax.nn.sigmoid(zg) @ rw_g2
    hd = lambda t: t.reshape(B, T, H, N)
    kk = hd(k * rw_kk)
    kk = kk / jnp.maximum(jnp.linalg.norm(kk, axis=-1, keepdims=True), 1e-12)
    k = k * (1.0 + (a - 1.0) * rw_ka)
    state, y = _wkv7_scan(wkv0, hd(r), hd(decay), hd(k), hd(v), -kk, kk * hd(a))
    mu = jnp.mean(y, -1, keepdims=True)
    var = jnp.mean(jnp.square(y - mu), -1, keepdims=True)
    yn = ((y - mu) * lax.rsqrt(var + RW_GN_EPS)).reshape(B, T, W) * rw_gn_g + rw_gn_b
    bonus = jnp.sum(hd(r) * hd(k) * rw_rk, -1, keepdims=True) * hd(v)
    out = (yn + bonus.reshape(B, T, W)) * g
    return out, state


def _indexer_topk(q_idx, w_idx, k_idx, tq, topk):
    L = k_idx.shape[1]
    scale = (q_idx.shape[2] * q_idx.shape[3]) ** -0.5
    s = jax.nn.relu(jnp.einsum('bqhd,bld->bqhl', q_idx, k_idx))
    score = jnp.einsum('bqh,bqhl->bql', w_idx, s) * scale
    causal = jnp.arange(L)[None, None, :] <= tq[:, :, None]
    _, idx = lax.top_k(jnp.where(causal, score, -jnp.inf), topk)
    return idx, idx <= tq[..., None]


def _sparse_attend(q, k_sel, v_sel, idx, valid, tq, slopes):
    logits = jnp.einsum('bqhd,bqkhd->bqhk', q, k_sel) * q.shape[-1] ** -0.5
    dist = (tq[..., None] - idx).astype(jnp.float32)[:, :, None, :]
    logits = jnp.where(valid[:, :, None, :], logits - slopes[:, None] * dist, -jnp.inf)
    p = jax.nn.softmax(logits, axis=-1)
    return jnp.einsum('bqhk,bqkhd->bqhd', p, v_sel)


def _prompt_sparse_attention(q, k, v, q_idx, k_idx, w_idx, slopes):
    B, S = q.shape[:2]
    topk = min(TOPK_MAX, S // 4)
    nb = S // Q_BLOCK
    blk = lambda t: jnp.moveaxis(t.reshape((B, nb, Q_BLOCK) + t.shape[2:]), 1, 0)
    gather = jax.vmap(lambda rows, ii: rows[ii])

    def one_block(inp):
        qb, qib, wib, t0 = inp
        tq = jnp.broadcast_to(t0 + jnp.arange(Q_BLOCK), (B, Q_BLOCK))
        idx, valid = _indexer_topk(qib, wib, k_idx, tq, topk)
        return _sparse_attend(qb, gather(k, idx), gather(v, idx), idx, valid, tq, slopes)

    out = lax.map(one_block, (blk(q), blk(q_idx), blk(w_idx), jnp.arange(nb) * Q_BLOCK))
    return jnp.moveaxis(out, 0, 1).reshape(B, S, -1)


def _sample_sparse_attention(q, k, v, q_idx, k_idx, w_idx, cache_k, cache_v, cache_kidx, page_table, slopes):
    DB, DS = q.shape[:2]
    n_pages = page_table.shape[1]
    past_len = n_pages * PAGE_SIZE
    L = past_len + DS
    topk = min(TOPK_MAX, L // 4)
    kidx_past = cache_kidx[page_table].reshape(DB, past_len, -1)
    kidx_all = jnp.concatenate([kidx_past, k_idx], axis=1)
    tq = past_len + jnp.broadcast_to(jnp.arange(DS), (DB, DS))
    idx, valid = _indexer_topk(q_idx, w_idx, kidx_all, tq, topk)
    in_past = idx < past_len
    pidx = jnp.minimum(idx, past_len - 1)
    bsel = jnp.arange(DB)[:, None, None]
    phys = page_table[bsel, pidx // PAGE_SIZE]
    slot = pidx % PAGE_SIZE
    nidx = jnp.clip(idx - past_len, 0, DS - 1)
    pick = lambda pool, new: jnp.where(in_past[..., None, None], pool[phys, slot], new[bsel, nidx])
    out = _sparse_attend(q, pick(cache_k, k), pick(cache_v, v), idx, valid, tq, slopes)
    return out.reshape(DB, DS, -1)


def _token_mixer(h, z_prev, wkv0, attn_fn, dims, w_in, rw_mu, rw_w0, rw_w2, rw_a0, rw_a2, rw_g2, rw_kk, rw_ka, rw_rk,
                 rw_gn_g, rw_gn_b, w_up_rw, w_up_at, w_out):
    B, T, _ = h.shape
    rw_in, at_in, at_heads, at_hd, idx_heads, idx_dim = dims
    at_w = at_heads * at_hd
    z = _mm3(h, w_in)
    z_rw, z_at, z_gate = jnp.split(z, [rw_in, rw_in + at_in], axis=-1)
    o_rw, wkv = _rwkv_branch(z_rw, z_prev, wkv0, rw_mu, rw_w0, rw_w2, rw_a0, rw_a2, rw_g2, rw_kk, rw_ka, rw_rk,
                             rw_gn_g, rw_gn_b)
    splits = [at_w, 2 * at_w, 3 * at_w, 3 * at_w + idx_heads * idx_dim, 3 * at_w + idx_heads * idx_dim + idx_dim]
    q, k, v, qi, ki, wi = jnp.split(z_at, splits, axis=-1)
    q = q.reshape(B, T, at_heads, at_hd)
    k = k.reshape(B, T, at_heads, at_hd)
    v = v.reshape(B, T, at_heads, at_hd)
    qi = qi.reshape(B, T, idx_heads, idx_dim)
    o_at = attn_fn(q, k, v, qi, ki, wi)
    g_rw, g_at = jnp.split(jax.nn.sigmoid(z_gate), 2, axis=-1)
    merged = g_rw * _mm3(o_rw, w_up_rw) + g_at * _mm3(o_at, w_up_at)
    return _mm3(merged, w_out), (k, v, ki, wkv, z_rw[:, -1])


def _peer_ffn(h, peer_wq, peer_keys, peer_u, peer_v):
    B, T, D = h.shape
    n = B * T
    heads, _, nkeys, half = peer_keys.shape
    pad = (-n) % TOKEN_BLOCK
    q_all = _matmul(h.reshape(n, D), peer_wq)
    q_blocks = jnp.pad(q_all, ((0, pad), (0, 0))).reshape(-1, TOKEN_BLOCK, q_all.shape[-1])
    blocks = jnp.pad(h.reshape(n, D), ((0, pad), (0, 0))).reshape(-1, TOKEN_BLOCK, D)

    def one(inp):
        xb, qb = inp
        q = qb.reshape(TOKEN_BLOCK, heads, 2, half)
        s = jnp.einsum('nhpd,hpkd->nhpk', q, peer_keys)
        sv, si = lax.top_k(s, PEER_TOPK)
        cand = sv[:, :, 0, :, None] + sv[:, :, 1, None, :]
        cs, ci = lax.top_k(cand.reshape(TOKEN_BLOCK, heads, PEER_TOPK * PEER_TOPK), PEER_TOPK)
        e = (jnp.take_along_axis(si[:, :, 0], ci // PEER_TOPK, -1) * nkeys
             + jnp.take_along_axis(si[:, :, 1], ci % PEER_TOPK, -1))
        gate = jax.nn.softmax(cs, axis=-1)
        act = jax.nn.gelu(jnp.einsum('nd,nhkd->nhk', xb, peer_u[e]), approximate=False)
        return jnp.einsum('nhk,nhkd->nd', gate * act, peer_v[e])

    out = lax.map(one, (blocks, q_blocks)).reshape(-1, D)[:n]
    return out.reshape(B, T, D)


def kernel(x_prompt, x_sample, c_prompt, c_sample, cache_k, cache_v, cache_kidx, state_wkv, state_shift, page_table,
           w_ada, b_ada, w_in, rw_mu, rw_w0, rw_w2, rw_a0, rw_a2, rw_g2, rw_kk, rw_ka, rw_rk, rw_gn_g, rw_gn_b,
           w_up_rw, w_up_at, w_out, ln1_g, ln1_b, peer_wq, peer_keys, peer_u, peer_v, ln2_g, ln2_b):
    depth = w_in.shape[0]
    at_heads, at_hd = cache_k.shape[3], cache_k.shape[4]
    idx_dim = cache_kidx.shape[3]
    rw_in = rw_mu.shape[1]
    gate_in = 2 * x_prompt.shape[-1]
    at_in = w_in.shape[2] - rw_in - gate_in
    idx_heads = (at_in - 3 * at_heads * at_hd - idx_dim) // (idx_dim + 1)
    dims = (rw_in, at_in, at_heads, at_hd, idx_heads, idx_dim)
    alpha = (2 * depth) ** 0.25
    slopes = 2.0 ** (-8.0 * jnp.arange(1, at_heads + 1, dtype=jnp.float32) / at_heads)

    yp, ys = x_prompt, x_sample
    Bp = x_prompt.shape[0]
    H, N = rw_rk.shape[1:]
    zero_shift = jnp.zeros((Bp, rw_in), x_prompt.dtype)
    zero_wkv = jnp.zeros((Bp, H, N, N), x_prompt.dtype)
    new_p = [[], [], [], [], []]
    new_s = [[], [], [], [], []]
    for l in range(depth):
        mp = _ada_mod(c_prompt, w_ada[l], b_ada[l])
        ms = _ada_mod(c_sample, w_ada[l], b_ada[l])
        mix_w = (w_in[l], rw_mu[l], rw_w0[l], rw_w2[l], rw_a0[l], rw_a2[l], rw_g2[l], rw_kk[l], rw_ka[l], rw_rk[l],
                 rw_gn_g[l], rw_gn_b[l], w_up_rw[l], w_up_at[l], w_out[l])
        prompt_attn = functools.partial(_prompt_sparse_attention, slopes=slopes)
        sample_attn = functools.partial(_sample_sparse_attention, cache_k=cache_k[l], cache_v=cache_v[l],
                                        cache_kidx=cache_kidx[l], page_table=page_table, slopes=slopes)
        o_p, st_p = _token_mixer(yp * (1 + mp[1]) + mp[0], zero_shift, zero_wkv, prompt_attn, dims, *mix_w)
        o_s, st_s = _token_mixer(ys * (1 + ms[1]) + ms[0], state_shift[l], state_wkv[l], sample_attn, dims, *mix_w)
        yp = _layer_norm(alpha * yp + mp[2] * o_p, ln1_g[l], ln1_b[l])
        ys = _layer_norm(alpha * ys + ms[2] * o_s, ln1_g[l], ln1_b[l])
        for i in range(5):
            new_p[i].append(st_p[i])
            new_s[i].append(st_s[i])
        peer_w = (peer_wq[l], peer_keys[l], peer_u[l], peer_v[l])
        yp = _layer_norm(alpha * yp + mp[5] * _peer_ffn(yp * (1 + mp[4]) + mp[3], *peer_w), ln2_g[l], ln2_b[l])
        ys = _layer_norm(alpha * ys + ms[5] * _peer_ffn(ys * (1 + ms[4]) + ms[3], *peer_w), ln2_g[l], ln2_b[l])
    k_p, v_p, kidx_p, wkv_p, shift_p = [jnp.stack(t) for t in new_p]
    k_s, v_s, kidx_s, wkv_s, shift_s = [jnp.stack(t) for t in new_s]
    return (yp, ys, k_p, v_p, kidx_p, wkv_p, shift_p, k_s, v_s, kidx_s, wkv_s, shift_s)
```

```python
import functools

import jax
import jax.numpy as jnp
from jax import lax
from jax.experimental import pallas as pl
from jax.experimental.pallas import tpu as pltpu

PAGE_SIZE = 128
TOPK_MAX = 256
Q_BLOCK = 128
PEER_TOPK = 16
TOKEN_BLOCK = 128
RW_GN_EPS = 64e-5
LN_EPS = 1e-5

MXU_DTYPE = jnp.bfloat16
LANES = 128
VMEM_LIMIT_BYTES = 56 * 1024 * 1024

ATT_Q_TILE = 256
ATT_K_TILE = 512

PEER_ROUTE_TILE = 256
PEER_TOKEN_TILE = 512
PEER_EXPERT_ROWS = 2


def _round_up(n, m):
    return (n + m - 1) // m * m


def _matmul_body(a_ref, b_ref, o_ref):
    o_ref[...] = jnp.dot(a_ref[...], b_ref[...], preferred_element_type=jnp.float32)


def _matmul(a, b, tm=512, tn=512):
    m, k = a.shape
    _, n = b.shape
    tm = min(tm, _round_up(m, 8))
    mp, np_ = _round_up(m, tm), _round_up(n, tn)
    a = a.astype(MXU_DTYPE)
    b = b.astype(MXU_DTYPE)
    if mp != m:
        a = jnp.pad(a, ((0, mp - m), (0, 0)))
    if np_ != n:
        b = jnp.pad(b, ((0, 0), (0, np_ - n)))
    out = pl.pallas_call(
        _matmul_body,
        grid=(mp // tm, np_ // tn),
        in_specs=[
            pl.BlockSpec((tm, k), lambda i, j: (i, 0)),
            pl.BlockSpec((k, tn), lambda i, j: (0, j)),
        ],
        out_specs=pl.BlockSpec((tm, tn), lambda i, j: (i, j)),
        out_shape=jax.ShapeDtypeStruct((mp, np_), jnp.float32),
        compiler_params=pltpu.CompilerParams(
            dimension_semantics=("arbitrary", "arbitrary"),
            vmem_limit_bytes=VMEM_LIMIT_BYTES,
        ),
        name="matmul",
    )(a, b)
    return out[:m, :n]


def _mm3(x, w):
    b, t, d = x.shape
    return _matmul(x.reshape(b * t, d), w).reshape(b, t, -1)


def _layer_norm(x, g, b):
    mu = jnp.mean(x, -1, keepdims=True)
    var = jnp.mean(jnp.square(x - mu), -1, keepdims=True)
    return (x - mu) * lax.rsqrt(var + LN_EPS) * g + b


def _ada_mod(c, w_ada, b_ada):
    m = _matmul(jax.nn.silu(c), w_ada, tn=1024) + b_ada
    return jnp.split(m[:, None, :], 6, axis=-1)


def _wkv7_scan(state, r, w, k, v, a, b):
    def step(S, inp):
        r_t, w_t, k_t, v_t, a_t, b_t = inp
        sa = jnp.einsum('bhvk,bhk->bhv', S, a_t)
        S = S * w_t[:, :, None, :] + sa[..., None] * b_t[:, :, None, :] + v_t[..., None] * k_t[:, :, None, :]
        return S, jnp.einsum('bhvk,bhk->bhv', S, r_t)
    xs = tuple(jnp.moveaxis(t, 1, 0) for t in (r, w, k, v, a, b))
    S, ys = lax.scan(step, state, xs)
    return S, jnp.moveaxis(ys, 0, 1)


def _rwkv_branch(z_rw, z_prev, wkv0, rw_mu, rw_w0, rw_w2, rw_a0, rw_a2, rw_g2, rw_kk, rw_ka, rw_rk, rw_gn_g, rw_gn_b):
    B, T, _ = z_rw.shape
    H, N = rw_rk.shape
    W = H * N
    d_lora, a_lora = rw_w2.shape[0], rw_a2.shape[0]
    splits = [W, 2 * W, 3 * W, 3 * W + d_lora, 3 * W + d_lora + a_lora]
    z_shift = jnp.concatenate([z_prev[:, None, :], z_rw[:, :-1]], axis=1)
    zm = z_rw + (z_shift - z_rw) * rw_mu
    r, k, v, zw, za, zg = jnp.split(zm, splits, axis=-1)
    logw = -jax.nn.softplus(-(rw_w0 + jnp.tanh(zw) @ rw_w2)) - 0.5
    decay = jnp.exp(-jnp.exp(logw))
    a = jax.nn.sigmoid(rw_a0 + za @ rw_a2)
    g = jax.nn.sigmoid(zg) @ rw_g2
    hd = lambda t: t.reshape(B, T, H, N)
    kk = hd(k * rw_kk)
    kk = kk / jnp.maximum(jnp.linalg.norm(kk, axis=-1, keepdims=True), 1e-12)
    k = k * (1.0 + (a - 1.0) * rw_ka)
    state, y = _wkv7_scan(wkv0, hd(r), hd(decay), hd(k), hd(v), -kk, kk * hd(a))
    mu = jnp.mean(y, -1, keepdims=True)
    var = jnp.mean(jnp.square(y - mu), -1, keepdims=True)
    yn = ((y - mu) * lax.rsqrt(var + RW_GN_EPS)).reshape(B, T, W) * rw_gn_g + rw_gn_b
    bonus = jnp.sum(hd(r) * hd(k) * rw_rk, -1, keepdims=True) * hd(v)
    out = (yn + bonus.reshape(B, T, W)) * g
    return out, state


def _indexer_topk(q_idx, w_idx, k_idx, tq, topk):
    L = k_idx.shape[1]
    scale = (q_idx.shape[2] * q_idx.shape[3]) ** -0.5
    s = jax.nn.relu(jnp.einsum('bqhd,bld->bqhl', q_idx, k_idx))
    score = jnp.einsum('bqh,bqhl->bql', w_idx, s) * scale
    causal = jnp.arange(L)[None, None, :] <= tq[:, :, None]
    _, idx = lax.top_k(jnp.where(causal, score, -jnp.inf), topk)
    return idx, idx <= tq[..., None]


def _sparse_attend(q, k_sel, v_sel, idx, valid, tq, slopes):
    logits = jnp.einsum('bqhd,bqkhd->bqhk', q, k_sel) * q.shape[-1] ** -0.5
    dist = (tq[..., None] - idx).astype(jnp.float32)[:, :, None, :]
    logits = jnp.where(valid[:, :, None, :], logits - slopes[:, None] * dist, -jnp.inf)
    p = jax.nn.softmax(logits, axis=-1)
    return jnp.einsum('bqhk,bqkhd->bqhd', p, v_sel)


INT32_MIN = -2 ** 31
MASKED_LOGIT_FLOOR = -1e30


def _index_select_body(qi_ref, w_ref, ki_ref, bias_ref, keys_ref, *, idx_heads, idx_dim, tq, lk, topk, scale):
    t0 = pl.program_id(0) * tq
    nt = (t0 + tq + lk - 1) // lk
    n_tiles = bias_ref.shape[1]
    row = t0 + lax.broadcasted_iota(jnp.int32, (tq, lk), 0)
    lane = lax.broadcasted_iota(jnp.int32, (tq, lk), 1)

    def score_tile(kt, carry):
        k0 = pl.multiple_of(kt * lk, lk)
        k_tile = ki_ref[pl.ds(k0, lk), :]
        acc = jnp.zeros((tq, lk), jnp.float32)
        for h in range(idx_heads):
            s = lax.dot_general(qi_ref[:, h * idx_dim:(h + 1) * idx_dim], k_tile, (((1,), (1,)), ((), ())),
                                preferred_element_type=jnp.float32)
            acc = acc + w_ref[:, h:h + 1] * jnp.maximum(s, 0.0)
        bits = pltpu.bitcast(acc * scale, jnp.int32)
        key = jnp.where(bits < 0, bits ^ 0x7FFFFFFF, bits)
        keys_ref[kt] = jnp.where(lane + k0 <= row, key, INT32_MIN)
        return carry

    lax.fori_loop(0, nt, score_tile, 0)

    def bit_step(i, thr):
        cand = thr + lax.shift_left(jnp.int32(1), 31 - i)

        def count_tile(kt, c):
            hit = jnp.where(keys_ref[kt] >= cand, 1.0, 0.0)
            for j in range(lk // LANES):
                c = c + hit[:, j * LANES:(j + 1) * LANES]
            return c

        c = lax.fori_loop(0, nt, count_tile, jnp.zeros((tq, LANES), jnp.float32))
        return jnp.where(jnp.sum(c, axis=1, keepdims=True) >= topk, cand, thr)

    thr = lax.fori_loop(0, 32, bit_step, jnp.full((tq, 1), INT32_MIN, jnp.int32))
    thr = jnp.maximum(thr, INT32_MIN + 1)

    def bias_tile(kt, carry):
        bias_ref[0, kt] = jnp.where(keys_ref[kt] >= thr, 0.0, -jnp.inf).astype(bias_ref.dtype)
        return carry

    lax.fori_loop(0, nt, bias_tile, 0)

    def fill_tile(kt, carry):
        bias_ref[0, kt] = jnp.full((tq, lk), -jnp.inf, bias_ref.dtype)
        return carry

    lax.fori_loop(nt, n_tiles, fill_tile, 0)


def _masked_attn_body(q_ref, k_ref, v_ref, bias_ref, o_ref, m_ref, l_ref, acc_ref,
                      *, heads, hd, tq, lk, scale, slopes):
    qb, kt = pl.program_id(0), pl.program_id(1)
    last = ((qb + 1) * tq - 1) // lk

    @pl.when(kt == 0)
    def _():
        m_ref[...] = jnp.full(m_ref.shape, MASKED_LOGIT_FLOOR, jnp.float32)
        l_ref[...] = jnp.zeros_like(l_ref)
        acc_ref[...] = jnp.zeros_like(acc_ref)

    @pl.when(kt <= last)
    def _():
        bias = bias_ref[0, 0].astype(jnp.float32)
        rel = ((kt * lk + lax.broadcasted_iota(jnp.int32, (tq, lk), 1))
               - (qb * tq + lax.broadcasted_iota(jnp.int32, (tq, lk), 0))).astype(jnp.float32)
        for h in range(heads):
            cols = slice(h * hd, (h + 1) * hd)
            logits = lax.dot_general(q_ref[:, cols], k_ref[:, cols], (((1,), (1,)), ((), ())),
                                     preferred_element_type=jnp.float32) * scale
            logits = logits + (slopes[h] * rel + bias)
            m_old = m_ref[h]
            m_new = jnp.maximum(m_old, jnp.max(logits, axis=1, keepdims=True))
            alpha = jnp.exp(m_old - m_new)
            p = jnp.exp(logits - pltpu.repeat(m_new, lk // LANES, axis=1))
            l_ref[h] = alpha * l_ref[h] + jnp.sum(p, axis=1, keepdims=True)
            m_ref[h] = m_new
            pv = jnp.dot(p.astype(v_ref.dtype), v_ref[:, cols], preferred_element_type=jnp.float32)
            acc_ref[:, cols] = pltpu.repeat(alpha, hd // LANES, axis=1) * acc_ref[:, cols] + pv

    @pl.when(kt == pl.num_programs(1) - 1)
    def _():
        for h in range(heads):
            cols = slice(h * hd, (h + 1) * hd)
            o_ref[:, cols] = acc_ref[:, cols] / pltpu.repeat(l_ref[h], hd // LANES, axis=1)


def _alibi_slopes(heads):
    return tuple(float(2.0 ** (-8.0 * (i + 1) / heads)) for i in range(heads))


def _prompt_sparse_attention(q, k, v, q_idx, k_idx, w_idx):
    B, S, heads, hd = q.shape
    idx_heads, idx_dim = q_idx.shape[2:]
    assert B == 1 and hd % LANES == 0
    topk = min(TOPK_MAX, S // 4)
    tq, lk = min(ATT_Q_TILE, S), min(ATT_K_TILE, S)
    assert S % tq == 0 and S % lk == 0 and lk % LANES == 0
    nq, nk = S // tq, S // lk

    bias = pl.pallas_call(
        functools.partial(_index_select_body, idx_heads=idx_heads, idx_dim=idx_dim, tq=tq, lk=lk, topk=topk,
                          scale=(idx_heads * idx_dim) ** -0.5),
        grid=(nq,),
        in_specs=[
            pl.BlockSpec((tq, idx_heads * idx_dim), lambda i: (i, 0)),
            pl.BlockSpec((tq, idx_heads), lambda i: (i, 0)),
            pl.BlockSpec((S, idx_dim), lambda i: (0, 0)),
        ],
        out_specs=pl.BlockSpec((1, nk, tq, lk), lambda i: (i, 0, 0, 0)),
        out_shape=jax.ShapeDtypeStruct((nq, nk, tq, lk), jnp.bfloat16),
        scratch_shapes=[pltpu.VMEM((nk, tq, lk), jnp.int32)],
        compiler_params=pltpu.CompilerParams(dimension_semantics=("arbitrary",),
                                             vmem_limit_bytes=VMEM_LIMIT_BYTES),
        name="index_select",
    )(q_idx.reshape(S, idx_heads * idx_dim).astype(MXU_DTYPE), w_idx.reshape(S, idx_heads),
      k_idx.reshape(S, idx_dim).astype(MXU_DTYPE))

    last = lambda i: ((i + 1) * tq - 1) // lk
    width = heads * hd
    out = pl.pallas_call(
        functools.partial(_masked_attn_body, heads=heads, hd=hd, tq=tq, lk=lk, scale=hd ** -0.5,
                          slopes=_alibi_slopes(heads)),
        grid=(nq, nk),
        in_specs=[
            pl.BlockSpec((tq, width), lambda i, j: (i, 0)),
            pl.BlockSpec((lk, width), lambda i, j: (jnp.minimum(j, last(i)), 0)),
            pl.BlockSpec((lk, width), lambda i, j: (jnp.minimum(j, last(i)), 0)),
            pl.BlockSpec((1, 1, tq, lk), lambda i, j: (i, jnp.minimum(j, last(i)), 0, 0)),
        ],
        out_specs=pl.BlockSpec((tq, width), lambda i, j: (i, 0)),
        out_shape=jax.ShapeDtypeStruct((S, width), jnp.float32),
        scratch_shapes=[pltpu.VMEM((heads, tq, LANES), jnp.float32), pltpu.VMEM((heads, tq, LANES), jnp.float32),
                        pltpu.VMEM((tq, width), jnp.float32)],
        compiler_params=pltpu.CompilerParams(dimension_semantics=("arbitrary", "arbitrary"),
                                             vmem_limit_bytes=VMEM_LIMIT_BYTES),
        name="masked_attention",
    )(q.reshape(S, width).astype(MXU_DTYPE), k.reshape(S, width).astype(MXU_DTYPE),
      v.reshape(S, width).astype(MXU_DTYPE), bias)
    return out.reshape(B, S, width)


def _sample_sparse_attention(q, k, v, q_idx, k_idx, w_idx, cache_k, cache_v, cache_kidx, page_table, slopes):
    DB, DS = q.shape[:2]
    n_pages = page_table.shape[1]
    past_len = n_pages * PAGE_SIZE
    L = past_len + DS
    topk = min(TOPK_MAX, L // 4)
    kidx_past = cache_kidx[page_table].reshape(DB, past_len, -1)
    kidx_all = jnp.concatenate([kidx_past, k_idx], axis=1)
    tq = past_len + jnp.broadcast_to(jnp.arange(DS), (DB, DS))
    idx, valid = _indexer_topk(q_idx, w_idx, kidx_all, tq, topk)
    in_past = idx < past_len
    pidx = jnp.minimum(idx, past_len - 1)
    bsel = jnp.arange(DB)[:, None, None]
    phys = page_table[bsel, pidx // PAGE_SIZE]
    slot = pidx % PAGE_SIZE
    nidx = jnp.clip(idx - past_len, 0, DS - 1)
    pick = lambda pool, new: jnp.where(in_past[..., None, None], pool[phys, slot], new[bsel, nidx])
    out = _sparse_attend(q, pick(cache_k, k), pick(cache_v, v), idx, valid, tq, slopes)
    return out.reshape(DB, DS, -1)


def _token_mixer(h, z_prev, wkv0, attn_fn, dims, w_in, rw_mu, rw_w0, rw_w2, rw_a0, rw_a2, rw_g2, rw_kk, rw_ka, rw_rk,
                 rw_gn_g, rw_gn_b, w_up_rw, w_up_at, w_out):
    B, T, _ = h.shape
    rw_in, at_in, at_heads, at_hd, idx_heads, idx_dim = dims
    at_w = at_heads * at_hd
    z = _mm3(h, w_in)
    z_rw, z_at, z_gate = jnp.split(z, [rw_in, rw_in + at_in], axis=-1)
    o_rw, wkv = _rwkv_branch(z_rw, z_prev, wkv0, rw_mu, rw_w0, rw_w2, rw_a0, rw_a2, rw_g2, rw_kk, rw_ka, rw_rk,
                             rw_gn_g, rw_gn_b)
    splits = [at_w, 2 * at_w, 3 * at_w, 3 * at_w + idx_heads * idx_dim, 3 * at_w + idx_heads * idx_dim + idx_dim]
    q, k, v, qi, ki, wi = jnp.split(z_at, splits, axis=-1)
    q = q.reshape(B, T, at_heads, at_hd)
    k = k.reshape(B, T, at_heads, at_hd)
    v = v.reshape(B, T, at_heads, at_hd)
    qi = qi.reshape(B, T, idx_heads, idx_dim)
    o_at = attn_fn(q, k, v, qi, ki, wi)
    g_rw, g_at = jnp.split(jax.nn.sigmoid(z_gate), 2, axis=-1)
    merged = g_rw * _mm3(o_rw, w_up_rw) + g_at * _mm3(o_at, w_up_at)
    return _mm3(merged, w_out), (k, v, ki, wkv, z_rw[:, -1])


def _top_values(s, k):
    n = s.shape[0]
    rows = lax.broadcasted_iota(jnp.int32, s.shape, 0)
    vals = []
    for _ in range(k):
        m = jnp.max(s, axis=0, keepdims=True)
        first = jnp.min(jnp.where(s == m, rows, n), axis=0, keepdims=True)
        s = jnp.where(rows == first, -jnp.inf, s)
        vals.append(m)
    return jnp.concatenate(vals, axis=0)


def _peer_route_body(q_ref, keys_ref, s1_ref, e1_ref, s2_ref, e2_ref, thr_ref, *, heads, half, topk):
    for h in range(heads):
        s = []
        for p in range(2):
            c0 = (2 * h + p) * half
            qhp = q_ref[:, c0:c0 + half].astype(MXU_DTYPE)
            s.append(lax.dot_general(keys_ref[h, p], qhp, (((1,), (1,)), ((), ())),
                                     preferred_element_type=jnp.float32))
        a = _top_values(s[0], topk)
        b = _top_values(s[1], topk)
        cand = jnp.concatenate([a[i:i + 1] + b for i in range(topk)], axis=0)
        cs = _top_values(cand, topk)
        z = jnp.sum(jnp.exp(cs - cs[0:1]), axis=0, keepdims=True)
        s1_ref[h] = s[0]
        s2_ref[h] = s[1]
        e1_ref[h] = jnp.exp(s[0] - a[0:1])
        e2_ref[h] = jnp.exp(s[1] - b[0:1]) / z
        thr_ref[h] = cs[topk - 1:topk]


def _gelu_exact(x):
    return 0.5 * x * (1.0 + lax.erf(x * (0.5 ** 0.5)))


def _peer_main_body(x_ref, u_ref, v_ref, s1_ref, e1_ref, s2_ref, e2_ref, thr_ref, o_ref, act_ref, wt_ref,
                    *, heads, nkeys, rows, lane_chunk):
    @pl.when(pl.program_id(1) == 0)
    def _():
        o_ref[...] = jnp.zeros_like(o_ref)

    t = x_ref.shape[0]
    act_ref[...] = lax.dot_general(u_ref[...], x_ref[...], (((1,), (1,)), ((), ())),
                                   preferred_element_type=jnp.float32)
    for il in range(rows):
        for c in range(t // lane_chunk):
            sl = slice(c * lane_chunk, (c + 1) * lane_chunk)
            g = jnp.zeros((nkeys, lane_chunk), jnp.float32)
            for h in range(heads):
                tot = s1_ref[h, 0, il:il + 1, sl] + s2_ref[h, :, sl]
                val = e1_ref[h, 0, il:il + 1, sl] * e2_ref[h, :, sl]
                g = g + jnp.where(tot >= thr_ref[h, :, sl], val, 0.0)
            w = g * _gelu_exact(act_ref[il * nkeys:(il + 1) * nkeys, sl])
            wt_ref[il * nkeys:(il + 1) * nkeys, sl] = w.astype(wt_ref.dtype)
    o_ref[...] += lax.dot_general(wt_ref[...], v_ref[...], (((0,), (0,)), ((), ())),
                                  preferred_element_type=jnp.float32)


def _peer_ffn(h, peer_wq, peer_keys, peer_u, peer_v):
    B, T, D = h.shape
    n = B * T
    heads, _, nkeys, half = peer_keys.shape
    experts = peer_u.shape[0]
    tok = min(PEER_TOKEN_TILE, n)
    rows = PEER_EXPERT_ROWS
    assert n % tok == 0 and nkeys % rows == 0 and experts == nkeys * nkeys
    x = h.reshape(n, D)
    q_all = _matmul(x, peer_wq)

    rt = min(PEER_ROUTE_TILE, n)
    tl_shape = jax.ShapeDtypeStruct((heads, nkeys, n), jnp.float32)
    tl_spec = pl.BlockSpec((heads, nkeys, rt), lambda i: (0, 0, i))
    s1, e1, s2, e2, thr = pl.pallas_call(
        functools.partial(_peer_route_body, heads=heads, half=half, topk=PEER_TOPK),
        grid=(n // rt,),
        in_specs=[
            pl.BlockSpec((rt, q_all.shape[1]), lambda i: (i, 0)),
            pl.BlockSpec(peer_keys.shape, lambda i: (0, 0, 0, 0)),
        ],
        out_specs=[tl_spec, tl_spec, tl_spec, tl_spec, pl.BlockSpec((heads, 1, rt), lambda i: (0, 0, i))],
        out_shape=[tl_shape, tl_shape, tl_shape, tl_shape, jax.ShapeDtypeStruct((heads, 1, n), jnp.float32)],
        compiler_params=pltpu.CompilerParams(dimension_semantics=("arbitrary",),
                                             vmem_limit_bytes=VMEM_LIMIT_BYTES),
        name="peer_route",
    )(q_all, peer_keys.astype(MXU_DTYPE))

    eb = rows * nkeys
    s1 = s1.reshape(heads, nkeys // rows, rows, n)
    e1 = e1.reshape(heads, nkeys // rows, rows, n)
    row_spec = pl.BlockSpec((heads, 1, rows, tok), lambda i, j: (0, j, 0, i))
    full_spec = pl.BlockSpec((heads, nkeys, tok), lambda i, j: (0, 0, i))
    out = pl.pallas_call(
        functools.partial(_peer_main_body, heads=heads, nkeys=nkeys, rows=rows,
                          lane_chunk=min(LANES, tok)),
        grid=(n // tok, experts // eb),
        in_specs=[
            pl.BlockSpec((tok, D), lambda i, j: (i, 0)),
            pl.BlockSpec((eb, D), lambda i, j: (j, 0)),
            pl.BlockSpec((eb, D), lambda i, j: (j, 0)),
            row_spec, row_spec, full_spec, full_spec,
            pl.BlockSpec((heads, 1, tok), lambda i, j: (0, 0, i)),
        ],
        out_specs=pl.BlockSpec((tok, D), lambda i, j: (i, 0)),
        out_shape=jax.ShapeDtypeStruct((n, D), jnp.float32),
        scratch_shapes=[pltpu.VMEM((eb, tok), jnp.float32), pltpu.VMEM((eb, tok), MXU_DTYPE)],
        compiler_params=pltpu.CompilerParams(dimension_semantics=("arbitrary", "arbitrary"),
                                             vmem_limit_bytes=VMEM_LIMIT_BYTES),
        name="peer_main",
    )(x.astype(MXU_DTYPE), peer_u.astype(MXU_DTYPE), peer_v.astype(MXU_DTYPE), s1, e1, s2, e2, thr)
    return out.reshape(B, T, D)


def kernel(x_prompt, x_sample, c_prompt, c_sample, cache_k, cache_v, cache_kidx, state_wkv, state_shift, page_table,
           w_ada, b_ada, w_in, rw_mu, rw_w0, rw_w2, rw_a0, rw_a2, rw_g2, rw_kk, rw_ka, rw_rk, rw_gn_g, rw_gn_b,
           w_up_rw, w_up_at, w_out, ln1_g, ln1_b, peer_wq, peer_keys, peer_u, peer_v, ln2_g, ln2_b):
    depth = w_in.shape[0]
    at_heads, at_hd = cache_k.shape[3], cache_k.shape[4]
    idx_dim = cache_kidx.shape[3]
    rw_in = rw_mu.shape[1]
    gate_in = 2 * x_prompt.shape[-1]
    at_in = w_in.shape[2] - rw_in - gate_in
    idx_heads = (at_in - 3 * at_heads * at_hd - idx_dim) // (idx_dim + 1)
    dims = (rw_in, at_in, at_heads, at_hd, idx_heads, idx_dim)
    alpha = (2 * depth) ** 0.25
    slopes = 2.0 ** (-8.0 * jnp.arange(1, at_heads + 1, dtype=jnp.float32) / at_heads)

    yp, ys = x_prompt, x_sample
    Bp = x_prompt.shape[0]
    H, N = rw_rk.shape[1:]
    zero_shift = jnp.zeros((Bp, rw_in), x_prompt.dtype)
    zero_wkv = jnp.zeros((Bp, H, N, N), x_prompt.dtype)
    new_p = [[], [], [], [], []]
    new_s = [[], [], [], [], []]
    for l in range(depth):
        mp = _ada_mod(c_prompt, w_ada[l], b_ada[l])
        ms = _ada_mod(c_sample, w_ada[l], b_ada[l])
        mix_w = (w_in[l], rw_mu[l], rw_w0[l], rw_w2[l], rw_a0[l], rw_a2[l], rw_g2[l], rw_kk[l], rw_ka[l], rw_rk[l],
                 rw_gn_g[l], rw_gn_b[l], w_up_rw[l], w_up_at[l], w_out[l])
        prompt_attn = _prompt_sparse_attention
        sample_attn = functools.partial(_sample_sparse_attention, cache_k=cache_k[l], cache_v=cache_v[l],
                                        cache_kidx=cache_kidx[l], page_table=page_table, slopes=slopes)
        o_p, st_p = _token_mixer(yp * (1 + mp[1]) + mp[0], zero_shift, zero_wkv, prompt_attn, dims, *mix_w)
        o_s, st_s = _token_mixer(ys * (1 + ms[1]) + ms[0], state_shift[l], state_wkv[l], sample_attn, dims, *mix_w)
        yp = _layer_norm(alpha * yp + mp[2] * o_p, ln1_g[l], ln1_b[l])
        ys = _layer_norm(alpha * ys + ms[2] * o_s, ln1_g[l], ln1_b[l])
        for i in range(5):
            new_p[i].append(st_p[i])
            new_s[i].append(st_s[i])
        peer_w = (peer_wq[l], peer_keys[l], peer_u[l], peer_v[l])
        yp = _layer_norm(alpha * yp + mp[5] * _peer_ffn(yp * (1 + mp[4]) + mp[3], *peer_w), ln2_g[l], ln2_b[l])
        ys = _layer_norm(alpha * ys + ms[5] * _peer_ffn(ys * (1 + ms[4]) + ms[3], *peer_w), ln2_g[l], ln2_b[l])
    k_p, v_p, kidx_p, wkv_p, shift_p = [jnp.stack(t) for t in new_p]
    k_s, v_s, kidx_s, wkv_s, shift_s = [jnp.stack(t) for t in new_s]
    return (yp, ys, k_p, v_p, kidx_p, wkv_p, shift_p, k_s, v_s, kidx_s, wkv_s, shift_s)
```

```python
import functools

import jax
import jax.numpy as jnp
from jax import lax
from jax.experimental import pallas as pl
from jax.experimental.pallas import tpu as pltpu

PAGE_SIZE = 128
TOPK_MAX = 256
Q_BLOCK = 128
PEER_TOPK = 16
TOKEN_BLOCK = 128
RW_GN_EPS = 64e-5
LN_EPS = 1e-5

MXU_DTYPE = jnp.bfloat16
LANES = 128
VMEM_LIMIT_BYTES = 56 * 1024 * 1024

WKV_GROUP_WIDTH = 256
ATT_Q_TILE = 256
ATT_K_TILE = 512

PEER_ROUTE_TILE = 256
PEER_TOKEN_TILE = 512
PEER_EXPERT_ROWS = 2


def _round_up(n, m):
    return (n + m - 1) // m * m


def _matmul_body(a_ref, b_ref, o_ref):
    o_ref[...] = jnp.dot(a_ref[...], b_ref[...], preferred_element_type=jnp.float32)


def _matmul(a, b, tm=512, tn=512):
    m, k = a.shape
    _, n = b.shape
    tm = min(tm, _round_up(m, 8))
    mp, np_ = _round_up(m, tm), _round_up(n, tn)
    a = a.astype(MXU_DTYPE)
    b = b.astype(MXU_DTYPE)
    if mp != m:
        a = jnp.pad(a, ((0, mp - m), (0, 0)))
    if np_ != n:
        b = jnp.pad(b, ((0, 0), (0, np_ - n)))
    out = pl.pallas_call(
        _matmul_body,
        grid=(mp // tm, np_ // tn),
        in_specs=[
            pl.BlockSpec((tm, k), lambda i, j: (i, 0)),
            pl.BlockSpec((k, tn), lambda i, j: (0, j)),
        ],
        out_specs=pl.BlockSpec((tm, tn), lambda i, j: (i, j)),
        out_shape=jax.ShapeDtypeStruct((mp, np_), jnp.float32),
        compiler_params=pltpu.CompilerParams(
            dimension_semantics=("arbitrary", "arbitrary"),
            vmem_limit_bytes=VMEM_LIMIT_BYTES,
        ),
        name="matmul",
    )(a, b)
    return out[:m, :n]


def _mm3(x, w):
    b, t, d = x.shape
    return _matmul(x.reshape(b * t, d), w).reshape(b, t, -1)


def _layer_norm(x, g, b):
    mu = jnp.mean(x, -1, keepdims=True)
    var = jnp.mean(jnp.square(x - mu), -1, keepdims=True)
    return (x - mu) * lax.rsqrt(var + LN_EPS) * g + b


def _ada_mod(c, w_ada, b_ada):
    m = _matmul(jax.nn.silu(c), w_ada, tn=1024) + b_ada
    return jnp.split(m[:, None, :], 6, axis=-1)


def _wkv7_scan(state, r, w, k, v, a, b):
    def step(S, inp):
        r_t, w_t, k_t, v_t, a_t, b_t = inp
        sa = jnp.einsum('bhvk,bhk->bhv', S, a_t)
        S = S * w_t[:, :, None, :] + sa[..., None] * b_t[:, :, None, :] + v_t[..., None] * k_t[:, :, None, :]
        return S, jnp.einsum('bhvk,bhk->bhv', S, r_t)
    xs = tuple(jnp.moveaxis(t, 1, 0) for t in (r, w, k, v, a, b))
    S, ys = lax.scan(step, state, xs)
    return S, jnp.moveaxis(ys, 0, 1)


_NN = (((1,), (0,)), ((), ()))
_NT = (((1,), (1,)), ((), ()))
_TN = (((0,), (0,)), ((), ()))


def _split_hi_lo(x):
    hi = x.astype(MXU_DTYPE)
    return hi, (x - hi.astype(jnp.float32)).astype(MXU_DTYPE)


def _dot3(a, b, dims):
    ah, al = _split_hi_lo(a)
    bh, bl = _split_hi_lo(b)
    d = lambda x, y: lax.dot_general(x, y, dims, preferred_element_type=jnp.float32)
    return d(ah, bh) + (d(ah, bl) + d(al, bh))


def _wkv7_chunk_body(r_ref, ld_ref, k_ref, v_ref, a_ref, b_ref, y_ref, s_out_ref, s_ref, *, n):
    c, gw = r_ref.shape
    hpg = gw // n

    @pl.when(pl.program_id(1) == 0)
    def _():
        s_ref[...] = jnp.zeros_like(s_ref)

    row = lax.broadcasted_iota(jnp.int32, (c, gw), 0)
    col = lax.broadcasted_iota(jnp.int32, (c, gw), 1)
    col_head, col_in = col // n, col % n
    eye_wide = jnp.where(col_in == row, 1.0, 0.0)
    strict, incl = col_in < row, col_in <= row
    sq_r = lax.broadcasted_iota(jnp.int32, (gw, gw), 0)
    sq_c = lax.broadcasted_iota(jnp.int32, (gw, gw), 1)
    same_head = (sq_r // n) == (sq_c // n)
    eye_sq = jnp.where(sq_r == sq_c, 1.0, 0.0)
    tril = jnp.where(lax.broadcasted_iota(jnp.int32, (c, c), 1) <= lax.broadcasted_iota(jnp.int32, (c, c), 0),
                     1.0, 0.0)

    def bd(x):
        return jnp.concatenate([jnp.where(col_head == h, x, 0.0) for h in range(hpg)], axis=0)

    r, ld, k, v, a, b = (ref[...] for ref in (r_ref, ld_ref, k_ref, v_ref, a_ref, b_ref))
    cs = _dot3(tril, ld, _NN)
    p_in = jnp.exp(cs)
    p_inv = jnp.exp(-cs)
    at, bt, kt, rt = a * jnp.exp(cs - ld), b * p_inv, k * p_inv, r * p_in
    bd_b, bd_k, bd_v = bd(bt), bd(kt), bd(v)
    lab = jnp.where(strict, _dot3(at, bd_b, _NT), 0.0)
    lak = jnp.where(strict, _dot3(at, bd_k, _NT), 0.0)
    mrb = jnp.where(incl, _dot3(rt, bd_b, _NT), 0.0)
    mrk = jnp.where(incl, _dot3(rt, bd_k, _NT), 0.0)

    pw = lab
    t = eye_wide + pw
    bd_pw = bd(pw)
    steps = max(c - 1, 1).bit_length() - 1
    for _ in range(steps):
        pw = _dot3(pw, bd_pw, _NN)
        bd_pw = bd(pw)
        t = t + _dot3(t, bd_pw, _NN)

    w1 = _dot3(t, bd(at), _NN)
    x = _dot3(t, bd(_dot3(lak, bd_v, _NN)), _NN)
    s0 = s_ref[...]
    yw = rt + _dot3(mrb, bd(w1), _NN)
    y_ref[...] = _dot3(yw, s0, _NT) + (_dot3(mrb, bd(x), _NN) + _dot3(mrk, bd_v, _NN))
    p_end = p_in[c - 1:c, :]
    g_mat = (eye_sq + jnp.where(same_head, _dot3(w1, bt, _TN), 0.0)) * p_end
    h_mat = jnp.where(same_head, _dot3(x, bt, _TN) + _dot3(v, kt, _TN), 0.0) * p_end
    s_new = _dot3(s0, g_mat, _NN) + h_mat
    s_ref[...] = s_new
    s_out_ref[0] = s_new


def _wkv7_chunked(r, log_decay, k, v, a, b, n):
    t_len, width = r.shape
    heads = width // n
    gw = min(WKV_GROUP_WIDTH, width)
    assert t_len % n == 0 and width % gw == 0 and gw % n == 0
    groups, hpg = width // gw, gw // n
    spec = pl.BlockSpec((n, gw), lambda g, c: (c, g))
    y, s_bd = pl.pallas_call(
        functools.partial(_wkv7_chunk_body, n=n),
        grid=(groups, t_len // n),
        in_specs=[spec] * 6,
        out_specs=[spec, pl.BlockSpec((1, gw, gw), lambda g, c: (g, 0, 0))],
        out_shape=[jax.ShapeDtypeStruct((t_len, width), jnp.float32),
                   jax.ShapeDtypeStruct((groups, gw, gw), jnp.float32)],
        scratch_shapes=[pltpu.VMEM((gw, gw), jnp.float32)],
        compiler_params=pltpu.CompilerParams(dimension_semantics=("arbitrary", "arbitrary"),
                                             vmem_limit_bytes=VMEM_LIMIT_BYTES),
        name="wkv7_chunked",
    )(r, log_decay, k, v, a, b)
    s5 = s_bd.reshape(groups, hpg, n, hpg, n)
    state = jnp.stack([s5[:, h, :, h, :] for h in range(hpg)], axis=1)
    return y, state.reshape(heads, n, n)


def _rwkv_branch(z_rw, z_prev, wkv0, rw_mu, rw_w0, rw_w2, rw_a0, rw_a2, rw_g2, rw_kk, rw_ka, rw_rk, rw_gn_g, rw_gn_b):
    B, T, _ = z_rw.shape
    H, N = rw_rk.shape
    W = H * N
    d_lora, a_lora = rw_w2.shape[0], rw_a2.shape[0]
    splits = [W, 2 * W, 3 * W, 3 * W + d_lora, 3 * W + d_lora + a_lora]
    z_shift = jnp.concatenate([z_prev[:, None, :], z_rw[:, :-1]], axis=1)
    zm = z_rw + (z_shift - z_rw) * rw_mu
    r, k, v, zw, za, zg = jnp.split(zm, splits, axis=-1)
    logw = -jax.nn.softplus(-(rw_w0 + jnp.tanh(zw) @ rw_w2)) - 0.5
    log_decay = -jnp.exp(logw)
    a = jax.nn.sigmoid(rw_a0 + za @ rw_a2)
    g = jax.nn.sigmoid(zg) @ rw_g2
    hd = lambda t: t.reshape(B, T, H, N)
    kk = hd(k * rw_kk)
    kk = kk / jnp.maximum(jnp.linalg.norm(kk, axis=-1, keepdims=True), 1e-12)
    k = k * (1.0 + (a - 1.0) * rw_ka)
    if wkv0 is None:
        assert B == 1
        y, state = _wkv7_chunked(r[0], log_decay[0], k[0], v[0], (-kk).reshape(T, W), (kk * hd(a)).reshape(T, W), N)
        y, state = y.reshape(B, T, H, N), state[None]
    else:
        state, y = _wkv7_scan(wkv0, hd(r), hd(jnp.exp(log_decay)), hd(k), hd(v), -kk, kk * hd(a))
    mu = jnp.mean(y, -1, keepdims=True)
    var = jnp.mean(jnp.square(y - mu), -1, keepdims=True)
    yn = ((y - mu) * lax.rsqrt(var + RW_GN_EPS)).reshape(B, T, W) * rw_gn_g + rw_gn_b
    bonus = jnp.sum(hd(r) * hd(k) * rw_rk, -1, keepdims=True) * hd(v)
    out = (yn + bonus.reshape(B, T, W)) * g
    return out, state


def _indexer_topk(q_idx, w_idx, k_idx, tq, topk):
    L = k_idx.shape[1]
    scale = (q_idx.shape[2] * q_idx.shape[3]) ** -0.5
    s = jax.nn.relu(jnp.einsum('bqhd,bld->bqhl', q_idx, k_idx))
    score = jnp.einsum('bqh,bqhl->bql', w_idx, s) * scale
    causal = jnp.arange(L)[None, None, :] <= tq[:, :, None]
    _, idx = lax.top_k(jnp.where(causal, score, -jnp.inf), topk)
    return idx, idx <= tq[..., None]


def _sparse_attend(q, k_sel, v_sel, idx, valid, tq, slopes):
    logits = jnp.einsum('bqhd,bqkhd->bqhk', q, k_sel) * q.shape[-1] ** -0.5
    dist = (tq[..., None] - idx).astype(jnp.float32)[:, :, None, :]
    logits = jnp.where(valid[:, :, None, :], logits - slopes[:, None] * dist, -jnp.inf)
    p = jax.nn.softmax(logits, axis=-1)
    return jnp.einsum('bqhk,bqkhd->bqhd', p, v_sel)


INT32_MIN = -2 ** 31
MASKED_LOGIT_FLOOR = -1e30


def _index_select_body(qi_ref, w_ref, ki_ref, bias_ref, keys_ref, *, idx_heads, idx_dim, tq, lk, topk, scale):
    t0 = pl.program_id(0) * tq
    nt = (t0 + tq + lk - 1) // lk
    n_tiles = bias_ref.shape[1]
    row = t0 + lax.broadcasted_iota(jnp.int32, (tq, lk), 0)
    lane = lax.broadcasted_iota(jnp.int32, (tq, lk), 1)

    def score_tile(kt, carry):
        k0 = pl.multiple_of(kt * lk, lk)
        k_tile = ki_ref[pl.ds(k0, lk), :]
        acc = jnp.zeros((tq, lk), jnp.float32)
        for h in range(idx_heads):
            s = lax.dot_general(qi_ref[:, h * idx_dim:(h + 1) * idx_dim], k_tile, (((1,), (1,)), ((), ())),
                                preferred_element_type=jnp.float32)
            acc = acc + w_ref[:, h:h + 1] * jnp.maximum(s, 0.0)
        bits = pltpu.bitcast(acc * scale, jnp.int32)
        key = jnp.where(bits < 0, bits ^ 0x7FFFFFFF, bits)
        keys_ref[kt] = jnp.where(lane + k0 <= row, key, INT32_MIN)
        return carry

    lax.fori_loop(0, nt, score_tile, 0)

    def bit_step(i, thr):
        cand = thr + lax.shift_left(jnp.int32(1), 31 - i)

        def count_tile(kt, c):
            hit = jnp.where(keys_ref[kt] >= cand, 1.0, 0.0)
            for j in range(lk // LANES):
                c = c + hit[:, j * LANES:(j + 1) * LANES]
            return c

        c = lax.fori_loop(0, nt, count_tile, jnp.zeros((tq, LANES), jnp.float32))
        return jnp.where(jnp.sum(c, axis=1, keepdims=True) >= topk, cand, thr)

    thr = lax.fori_loop(0, 32, bit_step, jnp.full((tq, 1), INT32_MIN, jnp.int32))
    thr = jnp.maximum(thr, INT32_MIN + 1)

    def bias_tile(kt, carry):
        bias_ref[0, kt] = jnp.where(keys_ref[kt] >= thr, 0.0, -jnp.inf).astype(bias_ref.dtype)
        return carry

    lax.fori_loop(0, nt, bias_tile, 0)

    def fill_tile(kt, carry):
        bias_ref[0, kt] = jnp.full((tq, lk), -jnp.inf, bias_ref.dtype)
        return carry

    lax.fori_loop(nt, n_tiles, fill_tile, 0)


def _lane_tile(x, reps):
    return x if reps == 1 else jnp.concatenate([x] * reps, axis=1)


def _masked_attn_body(q_ref, k_ref, v_ref, bias_ref, o_ref, m_ref, l_ref, acc_ref,
                      *, heads, hd, tq, lk, scale, slopes):
    qb, kt = pl.program_id(0), pl.program_id(1)
    last = ((qb + 1) * tq - 1) // lk

    @pl.when(kt == 0)
    def _():
        m_ref[...] = jnp.full(m_ref.shape, MASKED_LOGIT_FLOOR, jnp.float32)
        l_ref[...] = jnp.zeros_like(l_ref)
        acc_ref[...] = jnp.zeros_like(acc_ref)

    @pl.when(kt <= last)
    def _():
        bias = bias_ref[0, 0].astype(jnp.float32)
        rel = ((kt * lk + lax.broadcasted_iota(jnp.int32, (tq, lk), 1))
               - (qb * tq + lax.broadcasted_iota(jnp.int32, (tq, lk), 0))).astype(jnp.float32)
        for h in range(heads):
            cols = slice(h * hd, (h + 1) * hd)
            logits = lax.dot_general(q_ref[:, cols], k_ref[:, cols], (((1,), (1,)), ((), ())),
                                     preferred_element_type=jnp.float32) * scale
            logits = logits + (slopes[h] * rel + bias)
            m_old = m_ref[h]
            m_new = jnp.maximum(m_old, jnp.max(logits, axis=1, keepdims=True))
            alpha = jnp.exp(m_old - m_new)
            p = jnp.exp(logits - _lane_tile(m_new, lk // LANES))
            l_ref[h] = alpha * l_ref[h] + jnp.sum(p, axis=1, keepdims=True)
            m_ref[h] = m_new
            pv = jnp.dot(p.astype(v_ref.dtype), v_ref[:, cols], preferred_element_type=jnp.float32)
            acc_ref[:, cols] = _lane_tile(alpha, hd // LANES) * acc_ref[:, cols] + pv

    @pl.when(kt == pl.num_programs(1) - 1)
    def _():
        for h in range(heads):
            cols = slice(h * hd, (h + 1) * hd)
            o_ref[:, cols] = acc_ref[:, cols] / _lane_tile(l_ref[h], hd // LANES)


def _alibi_slopes(heads):
    return tuple(float(2.0 ** (-8.0 * (i + 1) / heads)) for i in range(heads))


def _prompt_sparse_attention(q, k, v, q_idx, k_idx, w_idx):
    B, S, heads, hd = q.shape
    idx_heads, idx_dim = q_idx.shape[2:]
    assert B == 1 and hd % LANES == 0
    topk = min(TOPK_MAX, S // 4)
    tq, lk = min(ATT_Q_TILE, S), min(ATT_K_TILE, S)
    assert S % tq == 0 and S % lk == 0 and lk % LANES == 0
    nq, nk = S // tq, S // lk

    bias = pl.pallas_call(
        functools.partial(_index_select_body, idx_heads=idx_heads, idx_dim=idx_dim, tq=tq, lk=lk, topk=topk,
                          scale=(idx_heads * idx_dim) ** -0.5),
        grid=(nq,),
        in_specs=[
            pl.BlockSpec((tq, idx_heads * idx_dim), lambda i: (i, 0)),
            pl.BlockSpec((tq, idx_heads), lambda i: (i, 0)),
            pl.BlockSpec((S, idx_dim), lambda i: (0, 0)),
        ],
        out_specs=pl.BlockSpec((1, nk, tq, lk), lambda i: (i, 0, 0, 0)),
        out_shape=jax.ShapeDtypeStruct((nq, nk, tq, lk), jnp.bfloat16),
        scratch_shapes=[pltpu.VMEM((nk, tq, lk), jnp.int32)],
        compiler_params=pltpu.CompilerParams(dimension_semantics=("arbitrary",),
                                             vmem_limit_bytes=VMEM_LIMIT_BYTES),
        name="index_select",
    )(q_idx.reshape(S, idx_heads * idx_dim).astype(MXU_DTYPE), w_idx.reshape(S, idx_heads),
      k_idx.reshape(S, idx_dim).astype(MXU_DTYPE))

    last = lambda i: ((i + 1) * tq - 1) // lk
    width = heads * hd
    out = pl.pallas_call(
        functools.partial(_masked_attn_body, heads=heads, hd=hd, tq=tq, lk=lk, scale=hd ** -0.5,
                          slopes=_alibi_slopes(heads)),
        grid=(nq, nk),
        in_specs=[
            pl.BlockSpec((tq, width), lambda i, j: (i, 0)),
            pl.BlockSpec((lk, width), lambda i, j: (jnp.minimum(j, last(i)), 0)),
            pl.BlockSpec((lk, width), lambda i, j: (jnp.minimum(j, last(i)), 0)),
            pl.BlockSpec((1, 1, tq, lk), lambda i, j: (i, jnp.minimum(j, last(i)), 0, 0)),
        ],
        out_specs=pl.BlockSpec((tq, width), lambda i, j: (i, 0)),
        out_shape=jax.ShapeDtypeStruct((S, width), jnp.float32),
        scratch_shapes=[pltpu.VMEM((heads, tq, LANES), jnp.float32), pltpu.VMEM((heads, tq, LANES), jnp.float32),
                        pltpu.VMEM((tq, width), jnp.float32)],
        compiler_params=pltpu.CompilerParams(dimension_semantics=("arbitrary", "arbitrary"),
                                             vmem_limit_bytes=VMEM_LIMIT_BYTES),
        name="masked_attention",
    )(q.reshape(S, width).astype(MXU_DTYPE), k.reshape(S, width).astype(MXU_DTYPE),
      v.reshape(S, width).astype(MXU_DTYPE), bias)
    return out.reshape(B, S, width)


def _sample_sparse_attention(q, k, v, q_idx, k_idx, w_idx, cache_k, cache_v, cache_kidx, page_table, slopes):
    DB, DS = q.shape[:2]
    n_pages = page_table.shape[1]
    past_len = n_pages * PAGE_SIZE
    L = past_len + DS
    topk = min(TOPK_MAX, L // 4)
    kidx_past = cache_kidx[page_table].reshape(DB, past_len, -1)
    kidx_all = jnp.concatenate([kidx_past, k_idx], axis=1)
    tq = past_len + jnp.broadcast_to(jnp.arange(DS), (DB, DS))
    idx, valid = _indexer_topk(q_idx, w_idx, kidx_all, tq, topk)
    in_past = idx < past_len
    pidx = jnp.minimum(idx, past_len - 1)
    bsel = jnp.arange(DB)[:, None, None]
    phys = page_table[bsel, pidx // PAGE_SIZE]
    slot = pidx % PAGE_SIZE
    nidx = jnp.clip(idx - past_len, 0, DS - 1)
    pick = lambda pool, new: jnp.where(in_past[..., None, None], pool[phys, slot], new[bsel, nidx])
    out = _sparse_attend(q, pick(cache_k, k), pick(cache_v, v), idx, valid, tq, slopes)
    return out.reshape(DB, DS, -1)


def _token_mixer(h, z_prev, wkv0, attn_fn, dims, w_in, rw_mu, rw_w0, rw_w2, rw_a0, rw_a2, rw_g2, rw_kk, rw_ka, rw_rk,
                 rw_gn_g, rw_gn_b, w_up_rw, w_up_at, w_out):
    B, T, _ = h.shape
    rw_in, at_in, at_heads, at_hd, idx_heads, idx_dim = dims
    at_w = at_heads * at_hd
    z = _mm3(h, w_in)
    z_rw, z_at, z_gate = jnp.split(z, [rw_in, rw_in + at_in], axis=-1)
    o_rw, wkv = _rwkv_branch(z_rw, z_prev, wkv0, rw_mu, rw_w0, rw_w2, rw_a0, rw_a2, rw_g2, rw_kk, rw_ka, rw_rk,
                             rw_gn_g, rw_gn_b)
    splits = [at_w, 2 * at_w, 3 * at_w, 3 * at_w + idx_heads * idx_dim, 3 * at_w + idx_heads * idx_dim + idx_dim]
    q, k, v, qi, ki, wi = jnp.split(z_at, splits, axis=-1)
    q = q.reshape(B, T, at_heads, at_hd)
    k = k.reshape(B, T, at_heads, at_hd)
    v = v.reshape(B, T, at_heads, at_hd)
    qi = qi.reshape(B, T, idx_heads, idx_dim)
    o_at = attn_fn(q, k, v, qi, ki, wi)
    g_rw, g_at = jnp.split(jax.nn.sigmoid(z_gate), 2, axis=-1)
    merged = g_rw * _mm3(o_rw, w_up_rw) + g_at * _mm3(o_at, w_up_at)
    return _mm3(merged, w_out), (k, v, ki, wkv, z_rw[:, -1])


def _top_values(s, k):
    n = s.shape[0]
    rows = lax.broadcasted_iota(jnp.int32, s.shape, 0)
    vals = []
    for _ in range(k):
        m = jnp.max(s, axis=0, keepdims=True)
        first = jnp.min(jnp.where(s == m, rows, n), axis=0, keepdims=True)
        s = jnp.where(rows == first, -jnp.inf, s)
        vals.append(m)
    return jnp.concatenate(vals, axis=0)


def _peer_route_body(q_ref, keys_ref, s1_ref, e1_ref, s2_ref, e2_ref, thr_ref, *, heads, half, topk):
    for h in range(heads):
        s = []
        for p in range(2):
            c0 = (2 * h + p) * half
            qhp = q_ref[:, c0:c0 + half].astype(MXU_DTYPE)
            s.append(lax.dot_general(keys_ref[h, p], qhp, (((1,), (1,)), ((), ())),
                                     preferred_element_type=jnp.float32))
        a = _top_values(s[0], topk)
        b = _top_values(s[1], topk)
        cand = jnp.concatenate([a[i:i + 1] + b for i in range(topk)], axis=0)
        cs = _top_values(cand, topk)
        z = jnp.sum(jnp.exp(cs - cs[0:1]), axis=0, keepdims=True)
        s1_ref[h] = s[0]
        s2_ref[h] = s[1]
        e1_ref[h] = jnp.exp(s[0] - a[0:1])
        e2_ref[h] = jnp.exp(s[1] - b[0:1]) / z
        thr_ref[h] = cs[topk - 1:topk]


def _gelu_exact(x):
    return 0.5 * x * (1.0 + lax.erf(x * (0.5 ** 0.5)))


def _peer_main_body(x_ref, u_ref, v_ref, s1_ref, e1_ref, s2_ref, e2_ref, thr_ref, o_ref, act_ref, wt_ref,
                    *, heads, nkeys, rows, lane_chunk):
    @pl.when(pl.program_id(1) == 0)
    def _():
        o_ref[...] = jnp.zeros_like(o_ref)

    t = x_ref.shape[0]
    act_ref[...] = lax.dot_general(u_ref[...], x_ref[...], (((1,), (1,)), ((), ())),
                                   preferred_element_type=jnp.float32)
    for il in range(rows):
        for c in range(t // lane_chunk):
            sl = slice(c * lane_chunk, (c + 1) * lane_chunk)
            g = jnp.zeros((nkeys, lane_chunk), jnp.float32)
            for h in range(heads):
                tot = s1_ref[h, 0, il:il + 1, sl] + s2_ref[h, :, sl]
                val = e1_ref[h, 0, il:il + 1, sl] * e2_ref[h, :, sl]
                g = g + jnp.where(tot >= thr_ref[h, :, sl], val, 0.0)
            w = g * _gelu_exact(act_ref[il * nkeys:(il + 1) * nkeys, sl])
            wt_ref[il * nkeys:(il + 1) * nkeys, sl] = w.astype(wt_ref.dtype)
    o_ref[...] += lax.dot_general(wt_ref[...], v_ref[...], (((0,), (0,)), ((), ())),
                                  preferred_element_type=jnp.float32)


def _peer_ffn(h, peer_wq, peer_keys, peer_u, peer_v):
    B, T, D = h.shape
    n = B * T
    heads, _, nkeys, half = peer_keys.shape
    experts = peer_u.shape[0]
    tok = min(PEER_TOKEN_TILE, n)
    rows = PEER_EXPERT_ROWS
    assert n % tok == 0 and nkeys % rows == 0 and experts == nkeys * nkeys
    x = h.reshape(n, D)
    q_all = _matmul(x, peer_wq)

    rt = min(PEER_ROUTE_TILE, n)
    tl_shape = jax.ShapeDtypeStruct((heads, nkeys, n), jnp.float32)
    tl_spec = pl.BlockSpec((heads, nkeys, rt), lambda i: (0, 0, i))
    s1, e1, s2, e2, thr = pl.pallas_call(
        functools.partial(_peer_route_body, heads=heads, half=half, topk=PEER_TOPK),
        grid=(n // rt,),
        in_specs=[
            pl.BlockSpec((rt, q_all.shape[1]), lambda i: (i, 0)),
            pl.BlockSpec(peer_keys.shape, lambda i: (0, 0, 0, 0)),
        ],
        out_specs=[tl_spec, tl_spec, tl_spec, tl_spec, pl.BlockSpec((heads, 1, rt), lambda i: (0, 0, i))],
        out_shape=[tl_shape, tl_shape, tl_shape, tl_shape, jax.ShapeDtypeStruct((heads, 1, n), jnp.float32)],
        compiler_params=pltpu.CompilerParams(dimension_semantics=("arbitrary",),
                                             vmem_limit_bytes=VMEM_LIMIT_BYTES),
        name="peer_route",
    )(q_all, peer_keys.astype(MXU_DTYPE))

    eb = rows * nkeys
    s1 = s1.reshape(heads, nkeys // rows, rows, n)
    e1 = e1.reshape(heads, nkeys // rows, rows, n)
    row_spec = pl.BlockSpec((heads, 1, rows, tok), lambda i, j: (0, j, 0, i))
    full_spec = pl.BlockSpec((heads, nkeys, tok), lambda i, j: (0, 0, i))
    out = pl.pallas_call(
        functools.partial(_peer_main_body, heads=heads, nkeys=nkeys, rows=rows,
                          lane_chunk=min(LANES, tok)),
        grid=(n // tok, experts // eb),
        in_specs=[
            pl.BlockSpec((tok, D), lambda i, j: (i, 0)),
            pl.BlockSpec((eb, D), lambda i, j: (j, 0)),
            pl.BlockSpec((eb, D), lambda i, j: (j, 0)),
            row_spec, row_spec, full_spec, full_spec,
            pl.BlockSpec((heads, 1, tok), lambda i, j: (0, 0, i)),
        ],
        out_specs=pl.BlockSpec((tok, D), lambda i, j: (i, 0)),
        out_shape=jax.ShapeDtypeStruct((n, D), jnp.float32),
        scratch_shapes=[pltpu.VMEM((eb, tok), jnp.float32), pltpu.VMEM((eb, tok), MXU_DTYPE)],
        compiler_params=pltpu.CompilerParams(dimension_semantics=("arbitrary", "arbitrary"),
                                             vmem_limit_bytes=VMEM_LIMIT_BYTES),
        name="peer_main",
    )(x.astype(MXU_DTYPE), peer_u.astype(MXU_DTYPE), peer_v.astype(MXU_DTYPE), s1, e1, s2, e2, thr)
    return out.reshape(B, T, D)


def kernel(x_prompt, x_sample, c_prompt, c_sample, cache_k, cache_v, cache_kidx, state_wkv, state_shift, page_table,
           w_ada, b_ada, w_in, rw_mu, rw_w0, rw_w2, rw_a0, rw_a2, rw_g2, rw_kk, rw_ka, rw_rk, rw_gn_g, rw_gn_b,
           w_up_rw, w_up_at, w_out, ln1_g, ln1_b, peer_wq, peer_keys, peer_u, peer_v, ln2_g, ln2_b):
    depth = w_in.shape[0]
    at_heads, at_hd = cache_k.shape[3], cache_k.shape[4]
    idx_dim = cache_kidx.shape[3]
    rw_in = rw_mu.shape[1]
    gate_in = 2 * x_prompt.shape[-1]
    at_in = w_in.shape[2] - rw_in - gate_in
    idx_heads = (at_in - 3 * at_heads * at_hd - idx_dim) // (idx_dim + 1)
    dims = (rw_in, at_in, at_heads, at_hd, idx_heads, idx_dim)
    alpha = (2 * depth) ** 0.25
    slopes = 2.0 ** (-8.0 * jnp.arange(1, at_heads + 1, dtype=jnp.float32) / at_heads)

    yp, ys = x_prompt, x_sample
    Bp = x_prompt.shape[0]
    H, N = rw_rk.shape[1:]
    zero_shift = jnp.zeros((Bp, rw_in), x_prompt.dtype)
    zero_wkv = jnp.zeros((Bp, H, N, N), x_prompt.dtype)
    new_p = [[], [], [], [], []]
    new_s = [[], [], [], [], []]
    for l in range(depth):
        mp = _ada_mod(c_prompt, w_ada[l], b_ada[l])
        ms = _ada_mod(c_sample, w_ada[l], b_ada[l])
        mix_w = (w_in[l], rw_mu[l], rw_w0[l], rw_w2[l], rw_a0[l], rw_a2[l], rw_g2[l], rw_kk[l], rw_ka[l], rw_rk[l],
                 rw_gn_g[l], rw_gn_b[l], w_up_rw[l], w_up_at[l], w_out[l])
        prompt_attn = _prompt_sparse_attention
        sample_attn = functools.partial(_sample_sparse_attention, cache_k=cache_k[l], cache_v=cache_v[l],
                                        cache_kidx=cache_kidx[l], page_table=page_table, slopes=slopes)
        o_p, st_p = _token_mixer(yp * (1 + mp[1]) + mp[0], zero_shift, None, prompt_attn, dims, *mix_w)
        o_s, st_s = _token_mixer(ys * (1 + ms[1]) + ms[0], state_shift[l], state_wkv[l], sample_attn, dims, *mix_w)
        yp = _layer_norm(alpha * yp + mp[2] * o_p, ln1_g[l], ln1_b[l])
        ys = _layer_norm(alpha * ys + ms[2] * o_s, ln1_g[l], ln1_b[l])
        for i in range(5):
            new_p[i].append(st_p[i])
            new_s[i].append(st_s[i])
        peer_w = (peer_wq[l], peer_keys[l], peer_u[l], peer_v[l])
        yp = _layer_norm(alpha * yp + mp[5] * _peer_ffn(yp * (1 + mp[4]) + mp[3], *peer_w), ln2_g[l], ln2_b[l])
        ys = _layer_norm(alpha * ys + ms[5] * _peer_ffn(ys * (1 + ms[4]) + ms[3], *peer_w), ln2_g[l], ln2_b[l])
    k_p, v_p, kidx_p, wkv_p, shift_p = [jnp.stack(t) for t in new_p]
    k_s, v_s, kidx_s, wkv_s, shift_s = [jnp.stack(t) for t in new_s]
    return (yp, ys, k_p, v_p, kidx_p, wkv_p, shift_p, k_s, v_s, kidx_s, wkv_s, shift_s)
```

```python
import functools

import jax
import jax.numpy as jnp
from jax import lax
from jax.experimental import pallas as pl
from jax.experimental.pallas import tpu as pltpu

PAGE_SIZE = 128
TOPK_MAX = 256
Q_BLOCK = 128
PEER_TOPK = 16
TOKEN_BLOCK = 128
RW_GN_EPS = 64e-5
LN_EPS = 1e-5

MXU_DTYPE = jnp.bfloat16
LANES = 128
SUBLANES_BF16 = 16
VMEM_LIMIT_BYTES = 56 * 1024 * 1024

IN_PROJ_TILE = 1024
WKV_GROUP_WIDTH = 256
WKV_GROUPS_PER_STEP = 2
ATT_Q_TILE = 256
ATT_K_TILE = 512

PEER_ROUTE_TILE = 256
PEER_TOKEN_TILE = 512
PEER_EXPERT_ROWS = 4


def _round_up(n, m):
    return (n + m - 1) // m * m


def _matmul_body(a_ref, b_ref, o_ref):
    o_ref[...] = jnp.dot(a_ref[...], b_ref[...], preferred_element_type=jnp.float32)


def _matmul(a, b, tm=512, tn=512):
    m, k = a.shape
    _, n = b.shape
    tm, tn = min(tm, _round_up(m, SUBLANES_BF16)), min(tn, n)
    mp = _round_up(m, tm)
    a = a.astype(MXU_DTYPE)
    b = b.astype(MXU_DTYPE)
    if mp != m:
        a = jnp.pad(a, ((0, mp - m), (0, 0)))
    out = pl.pallas_call(
        _matmul_body,
        grid=(mp // tm, pl.cdiv(n, tn)),
        in_specs=[
            pl.BlockSpec((tm, k), lambda i, j: (i, 0)),
            pl.BlockSpec((k, tn), lambda i, j: (0, j)),
        ],
        out_specs=pl.BlockSpec((tm, tn), lambda i, j: (i, j)),
        out_shape=jax.ShapeDtypeStruct((mp, n), jnp.float32),
        compiler_params=pltpu.CompilerParams(
            dimension_semantics=("arbitrary", "arbitrary"),
            vmem_limit_bytes=VMEM_LIMIT_BYTES,
        ),
        name="matmul",
    )(a, b)
    return out if mp == m else out[:m]


def _mm3(x, w):
    b, t, d = x.shape
    return _matmul(x.reshape(b * t, d), w).reshape(b, t, -1)


def _layer_norm(x, g, b):
    mu = jnp.mean(x, -1, keepdims=True)
    var = jnp.mean(jnp.square(x - mu), -1, keepdims=True)
    return (x - mu) * lax.rsqrt(var + LN_EPS) * g + b


def _ada_mod(c_prompt, c_sample, w_ada, b_ada):
    c = jnp.concatenate([c_prompt, c_sample], axis=0)
    m = _matmul(jax.nn.silu(c), w_ada, tn=1024) + b_ada
    nb = c_prompt.shape[0]
    return jnp.split(m[:nb, None, :], 6, axis=-1), jnp.split(m[nb:, None, :], 6, axis=-1)


def _wkv7_scan(state, r, w, k, v, a, b):
    def step(S, inp):
        r_t, w_t, k_t, v_t, a_t, b_t = inp
        sa = jnp.einsum('bhvk,bhk->bhv', S, a_t)
        S = S * w_t[:, :, None, :] + sa[..., None] * b_t[:, :, None, :] + v_t[..., None] * k_t[:, :, None, :]
        return S, jnp.einsum('bhvk,bhk->bhv', S, r_t)
    xs = tuple(jnp.moveaxis(t, 1, 0) for t in (r, w, k, v, a, b))
    S, ys = lax.scan(step, state, xs)
    return S, jnp.moveaxis(ys, 0, 1)


_NN = (((1,), (0,)), ((), ()))
_NT = (((1,), (1,)), ((), ()))
_TN = (((0,), (0,)), ((), ()))


def _split_hi_lo(x):
    hi = x.astype(MXU_DTYPE)
    return hi, (x - hi.astype(jnp.float32)).astype(MXU_DTYPE)


def _dot3(a, b, dims):
    ah, al = _split_hi_lo(a)
    bh, bl = _split_hi_lo(b)
    d = lambda x, y: lax.dot_general(x, y, dims, preferred_element_type=jnp.float32)
    return d(ah, bh) + (d(ah, bl) + d(al, bh))


def _wkv7_chunk_body(r_ref, ld_ref, k_ref, v_ref, a_ref, b_ref, y_ref, s_out_ref, s_ref, *, n, gw):
    c = r_ref.shape[0]
    gps = r_ref.shape[1] // gw
    hpg = gw // n

    @pl.when(pl.program_id(1) == 0)
    def _():
        s_ref[...] = jnp.zeros_like(s_ref)

    row = lax.broadcasted_iota(jnp.int32, (c, gw), 0)
    col = lax.broadcasted_iota(jnp.int32, (c, gw), 1)
    col_head, col_in = col // n, col % n
    eye_wide = jnp.where(col_in == row, 1.0, 0.0)
    strict, incl = col_in < row, col_in <= row
    sq_r = lax.broadcasted_iota(jnp.int32, (gw, gw), 0)
    sq_c = lax.broadcasted_iota(jnp.int32, (gw, gw), 1)
    same_head = (sq_r // n) == (sq_c // n)
    eye_sq = jnp.where(sq_r == sq_c, 1.0, 0.0)
    tril = jnp.where(lax.broadcasted_iota(jnp.int32, (c, c), 1) <= lax.broadcasted_iota(jnp.int32, (c, c), 0),
                     1.0, 0.0)

    def bd(x):
        return jnp.concatenate([jnp.where(col_head == h, x, 0.0) for h in range(hpg)], axis=0)

    for j in range(gps):
        sl = slice(j * gw, (j + 1) * gw)
        r, ld, k, v, a, b = (ref[:, sl] for ref in (r_ref, ld_ref, k_ref, v_ref, a_ref, b_ref))
        cs = _dot3(tril, ld, _NN)
        p_in = jnp.exp(cs)
        p_inv = jnp.exp(-cs)
        at, bt, kt, rt = a * jnp.exp(cs - ld), b * p_inv, k * p_inv, r * p_in
        bd_b, bd_k, bd_v = bd(bt), bd(kt), bd(v)
        lab = jnp.where(strict, _dot3(at, bd_b, _NT), 0.0)
        lak = jnp.where(strict, _dot3(at, bd_k, _NT), 0.0)
        mrb = jnp.where(incl, _dot3(rt, bd_b, _NT), 0.0)
        mrk = jnp.where(incl, _dot3(rt, bd_k, _NT), 0.0)

        pw = lab
        t = eye_wide + pw
        bd_pw = bd(pw)
        for _ in range(max(c - 1, 1).bit_length() - 1):
            pw = _dot3(pw, bd_pw, _NN)
            bd_pw = bd(pw)
            t = t + _dot3(t, bd_pw, _NN)

        w1 = _dot3(t, bd(at), _NN)
        x = _dot3(t, bd(_dot3(lak, bd_v, _NN)), _NN)
        s0 = s_ref[j]
        yw = rt + _dot3(mrb, bd(w1), _NN)
        y_ref[:, sl] = _dot3(yw, s0, _NT) + (_dot3(mrb, bd(x), _NN) + _dot3(mrk, bd_v, _NN))
        p_end = p_in[c - 1:c, :]
        g_mat = (eye_sq + jnp.where(same_head, _dot3(w1, bt, _TN), 0.0)) * p_end
        h_mat = jnp.where(same_head, _dot3(x, bt, _TN) + _dot3(v, kt, _TN), 0.0) * p_end
        s_new = _dot3(s0, g_mat, _NN) + h_mat
        s_ref[j] = s_new
        s_out_ref[j] = s_new


def _wkv7_chunked(r, log_decay, k, v, a, b, n):
    t_len, width = r.shape
    heads = width // n
    gw = min(WKV_GROUP_WIDTH, width)
    assert t_len % n == 0 and width % gw == 0 and gw % n == 0
    groups, hpg = width // gw, gw // n
    gps = WKV_GROUPS_PER_STEP if groups % WKV_GROUPS_PER_STEP == 0 else 1
    spec = pl.BlockSpec((n, gps * gw), lambda g, c: (c, g))
    y, s_bd = pl.pallas_call(
        functools.partial(_wkv7_chunk_body, n=n, gw=gw),
        grid=(groups // gps, t_len // n),
        in_specs=[spec] * 6,
        out_specs=[spec, pl.BlockSpec((gps, gw, gw), lambda g, c: (g, 0, 0))],
        out_shape=[jax.ShapeDtypeStruct((t_len, width), jnp.float32),
                   jax.ShapeDtypeStruct((groups, gw, gw), jnp.float32)],
        scratch_shapes=[pltpu.VMEM((gps, gw, gw), jnp.float32)],
        compiler_params=pltpu.CompilerParams(dimension_semantics=("arbitrary", "arbitrary"),
                                             vmem_limit_bytes=VMEM_LIMIT_BYTES),
        name="wkv7_chunked",
    )(r, log_decay, k, v, a, b)
    s5 = s_bd.reshape(groups, hpg, n, hpg, n)
    state = jnp.stack([s5[:, h, :, h, :] for h in range(hpg)], axis=1)
    return y, state.reshape(heads, n, n)


def _rwkv_branch(z_rw, z_prev, wkv0, rw_mu, rw_w0, rw_w2, rw_a0, rw_a2, rw_g2, rw_kk, rw_ka, rw_rk, rw_gn_g, rw_gn_b):
    B, T, _ = z_rw.shape
    H, N = rw_rk.shape
    W = H * N
    d_lora, a_lora = rw_w2.shape[0], rw_a2.shape[0]
    splits = [W, 2 * W, 3 * W, 3 * W + d_lora, 3 * W + d_lora + a_lora]
    z_shift = jnp.concatenate([z_prev[:, None, :], z_rw[:, :-1]], axis=1)
    zm = z_rw + (z_shift - z_rw) * rw_mu
    r, k, v, zw, za, zg = jnp.split(zm, splits, axis=-1)
    logw = -jax.nn.softplus(-(rw_w0 + jnp.tanh(zw) @ rw_w2)) - 0.5
    log_decay = -jnp.exp(logw)
    a = jax.nn.sigmoid(rw_a0 + za @ rw_a2)
    g = jax.nn.sigmoid(zg) @ rw_g2
    hd = lambda t: t.reshape(B, T, H, N)
    kk = hd(k * rw_kk)
    kk = kk / jnp.maximum(jnp.linalg.norm(kk, axis=-1, keepdims=True), 1e-12)
    k = k * (1.0 + (a - 1.0) * rw_ka)
    if wkv0 is None:
        assert B == 1
        y, state = _wkv7_chunked(r[0], log_decay[0], k[0], v[0], (-kk).reshape(T, W), (kk * hd(a)).reshape(T, W), N)
        y, state = y.reshape(B, T, H, N), state[None]
    else:
        state, y = _wkv7_scan(wkv0, hd(r), hd(jnp.exp(log_decay)), hd(k), hd(v), -kk, kk * hd(a))
    mu = jnp.mean(y, -1, keepdims=True)
    var = jnp.mean(jnp.square(y - mu), -1, keepdims=True)
    yn = ((y - mu) * lax.rsqrt(var + RW_GN_EPS)).reshape(B, T, W) * rw_gn_g + rw_gn_b
    bonus = jnp.sum(hd(r) * hd(k) * rw_rk, -1, keepdims=True) * hd(v)
    out = (yn + bonus.reshape(B, T, W)) * g
    return out, state


def _indexer_topk(q_idx, w_idx, k_idx, tq, topk):
    L = k_idx.shape[1]
    scale = (q_idx.shape[2] * q_idx.shape[3]) ** -0.5
    s = jax.nn.relu(jnp.einsum('bqhd,bld->bqhl', q_idx, k_idx))
    score = jnp.einsum('bqh,bqhl->bql', w_idx, s) * scale
    causal = jnp.arange(L)[None, None, :] <= tq[:, :, None]
    _, idx = lax.top_k(jnp.where(causal, score, -jnp.inf), topk)
    return idx, idx <= tq[..., None]


def _sparse_attend(q, k_sel, v_sel, idx, valid, tq, slopes):
    logits = jnp.einsum('bqhd,bqkhd->bqhk', q, k_sel) * q.shape[-1] ** -0.5
    dist = (tq[..., None] - idx).astype(jnp.float32)[:, :, None, :]
    logits = jnp.where(valid[:, :, None, :], logits - slopes[:, None] * dist, -jnp.inf)
    p = jax.nn.softmax(logits, axis=-1)
    return jnp.einsum('bqhk,bqkhd->bqhd', p, v_sel)


INT32_MIN = -2 ** 31
MASKED_LOGIT_FLOOR = -1e30


def _index_select_body(qi_ref, w_ref, ki_ref, bias_ref, keys_ref, *, idx_heads, idx_dim, tq, lk, topk, scale):
    t0 = pl.program_id(0) * tq
    nt = (t0 + tq + lk - 1) // lk
    n_tiles = bias_ref.shape[1]
    row = t0 + lax.broadcasted_iota(jnp.int32, (tq, lk), 0)
    lane = lax.broadcasted_iota(jnp.int32, (tq, lk), 1)

    def score_tile(kt, carry):
        k0 = pl.multiple_of(kt * lk, lk)
        k_tile = ki_ref[pl.ds(k0, lk), :]
        acc = jnp.zeros((tq, lk), jnp.float32)
        for h in range(idx_heads):
            s = lax.dot_general(qi_ref[:, h * idx_dim:(h + 1) * idx_dim], k_tile, (((1,), (1,)), ((), ())),
                                preferred_element_type=jnp.float32)
            acc = acc + w_ref[:, h:h + 1] * jnp.maximum(s, 0.0)
        bits = pltpu.bitcast(acc * scale, jnp.int32)
        key = jnp.where(bits < 0, bits ^ 0x7FFFFFFF, bits)
        keys_ref[kt] = jnp.where(lane + k0 <= row, key, INT32_MIN)
        return carry

    lax.fori_loop(0, nt, score_tile, 0)

    def bit_step(i, thr):
        cand = thr + lax.shift_left(jnp.int32(1), 31 - i)

        def count_tile(kt, c):
            hit = jnp.where(keys_ref[kt] >= cand, 1.0, 0.0)
            for j in range(lk // LANES):
                c = c + hit[:, j * LANES:(j + 1) * LANES]
            return c

        c = lax.fori_loop(0, nt, count_tile, jnp.zeros((tq, LANES), jnp.float32))
        return jnp.where(jnp.sum(c, axis=1, keepdims=True) >= topk, cand, thr)

    thr = lax.fori_loop(0, 32, bit_step, jnp.full((tq, 1), INT32_MIN, jnp.int32))
    thr = jnp.maximum(thr, INT32_MIN + 1)

    def bias_tile(kt, carry):
        bias_ref[0, kt] = jnp.where(keys_ref[kt] >= thr, 0.0, -jnp.inf).astype(bias_ref.dtype)
        return carry

    lax.fori_loop(0, nt, bias_tile, 0)

    def fill_tile(kt, carry):
        bias_ref[0, kt] = jnp.full((tq, lk), -jnp.inf, bias_ref.dtype)
        return carry

    lax.fori_loop(nt, n_tiles, fill_tile, 0)


def _lane_tile(x, reps):
    return x if reps == 1 else jnp.concatenate([x] * reps, axis=1)


def _masked_attn_body(q_ref, k_ref, v_ref, bias_ref, o_ref, m_ref, l_ref, acc_ref,
                      *, heads, hd, tq, lk, scale, slopes):
    qb, kt = pl.program_id(0), pl.program_id(1)
    last = ((qb + 1) * tq - 1) // lk

    @pl.when(kt == 0)
    def _():
        m_ref[...] = jnp.full(m_ref.shape, MASKED_LOGIT_FLOOR, jnp.float32)
        l_ref[...] = jnp.zeros_like(l_ref)
        acc_ref[...] = jnp.zeros_like(acc_ref)

    @pl.when(kt <= last)
    def _():
        bias = bias_ref[0, 0].astype(jnp.float32)
        rel = ((kt * lk + lax.broadcasted_iota(jnp.int32, (tq, lk), 1))
               - (qb * tq + lax.broadcasted_iota(jnp.int32, (tq, lk), 0))).astype(jnp.float32)
        for h in range(heads):
            cols = slice(h * hd, (h + 1) * hd)
            logits = lax.dot_general(q_ref[:, cols], k_ref[:, cols], (((1,), (1,)), ((), ())),
                                     preferred_element_type=jnp.float32) * scale
            logits = logits + (slopes[h] * rel + bias)
            m_old = m_ref[h]
            m_new = jnp.maximum(m_old, jnp.max(logits, axis=1, keepdims=True))
            alpha = jnp.exp(m_old - m_new)
            p = jnp.exp(logits - _lane_tile(m_new, lk // LANES))
            l_ref[h] = alpha * l_ref[h] + jnp.sum(p, axis=1, keepdims=True)
            m_ref[h] = m_new
            pv = jnp.dot(p.astype(v_ref.dtype), v_ref[:, cols], preferred_element_type=jnp.float32)
            acc_ref[:, cols] = _lane_tile(alpha, hd // LANES) * acc_ref[:, cols] + pv

    @pl.when(kt == pl.num_programs(1) - 1)
    def _():
        for h in range(heads):
            cols = slice(h * hd, (h + 1) * hd)
            o_ref[:, cols] = acc_ref[:, cols] / _lane_tile(l_ref[h], hd // LANES)


def _alibi_slopes(heads):
    return tuple(float(2.0 ** (-8.0 * (i + 1) / heads)) for i in range(heads))


def _prompt_sparse_attention(q, k, v, q_idx, k_idx, w_idx):
    B, S, heads, hd = q.shape
    idx_heads, idx_dim = q_idx.shape[2:]
    assert B == 1 and hd % LANES == 0
    topk = min(TOPK_MAX, S // 4)
    tq, lk = min(ATT_Q_TILE, S), min(ATT_K_TILE, S)
    assert S % tq == 0 and S % lk == 0 and lk % LANES == 0
    nq, nk = S // tq, S // lk

    bias = pl.pallas_call(
        functools.partial(_index_select_body, idx_heads=idx_heads, idx_dim=idx_dim, tq=tq, lk=lk, topk=topk,
                          scale=(idx_heads * idx_dim) ** -0.5),
        grid=(nq,),
        in_specs=[
            pl.BlockSpec((tq, idx_heads * idx_dim), lambda i: (i, 0)),
            pl.BlockSpec((tq, idx_heads), lambda i: (i, 0)),
            pl.BlockSpec((S, idx_dim), lambda i: (0, 0)),
        ],
        out_specs=pl.BlockSpec((1, nk, tq, lk), lambda i: (i, 0, 0, 0)),
        out_shape=jax.ShapeDtypeStruct((nq, nk, tq, lk), jnp.bfloat16),
        scratch_shapes=[pltpu.VMEM((nk, tq, lk), jnp.int32)],
        compiler_params=pltpu.CompilerParams(dimension_semantics=("arbitrary",),
                                             vmem_limit_bytes=VMEM_LIMIT_BYTES),
        name="index_select",
    )(q_idx.reshape(S, idx_heads * idx_dim).astype(MXU_DTYPE), w_idx.reshape(S, idx_heads),
      k_idx.reshape(S, idx_dim).astype(MXU_DTYPE))

    last = lambda i: ((i + 1) * tq - 1) // lk
    width = heads * hd
    out = pl.pallas_call(
        functools.partial(_masked_attn_body, heads=heads, hd=hd, tq=tq, lk=lk, scale=hd ** -0.5,
                          slopes=_alibi_slopes(heads)),
        grid=(nq, nk),
        in_specs=[
            pl.BlockSpec((tq, width), lambda i, j: (i, 0)),
            pl.BlockSpec((lk, width), lambda i, j: (jnp.minimum(j, last(i)), 0)),
            pl.BlockSpec((lk, width), lambda i, j: (jnp.minimum(j, last(i)), 0)),
            pl.BlockSpec((1, 1, tq, lk), lambda i, j: (i, jnp.minimum(j, last(i)), 0, 0)),
        ],
        out_specs=pl.BlockSpec((tq, width), lambda i, j: (i, 0)),
        out_shape=jax.ShapeDtypeStruct((S, width), jnp.float32),
        scratch_shapes=[pltpu.VMEM((heads, tq, LANES), jnp.float32), pltpu.VMEM((heads, tq, LANES), jnp.float32),
                        pltpu.VMEM((tq, width), jnp.float32)],
        compiler_params=pltpu.CompilerParams(dimension_semantics=("arbitrary", "arbitrary"),
                                             vmem_limit_bytes=VMEM_LIMIT_BYTES),
        name="masked_attention",
    )(q.reshape(S, width).astype(MXU_DTYPE), k.reshape(S, width).astype(MXU_DTYPE),
      v.reshape(S, width).astype(MXU_DTYPE), bias)
    return out.reshape(B, S, width)


def _sample_sparse_attention(q, k, v, q_idx, k_idx, w_idx, cache_k, cache_v, cache_kidx, page_table, slopes):
    DB, DS = q.shape[:2]
    n_pages = page_table.shape[1]
    past_len = n_pages * PAGE_SIZE
    L = past_len + DS
    topk = min(TOPK_MAX, L // 4)
    kidx_past = cache_kidx[page_table].reshape(DB, past_len, -1)
    kidx_all = jnp.concatenate([kidx_past, k_idx], axis=1)
    tq = past_len + jnp.broadcast_to(jnp.arange(DS), (DB, DS))
    idx, valid = _indexer_topk(q_idx, w_idx, kidx_all, tq, topk)
    in_past = idx < past_len
    pidx = jnp.minimum(idx, past_len - 1)
    bsel = jnp.arange(DB)[:, None, None]
    phys = page_table[bsel, pidx // PAGE_SIZE]
    slot = pidx % PAGE_SIZE
    nidx = jnp.clip(idx - past_len, 0, DS - 1)
    pick = lambda pool, new: jnp.where(in_past[..., None, None], pool[phys, slot], new[bsel, nidx])
    out = _sparse_attend(q, pick(cache_k, k), pick(cache_v, v), idx, valid, tq, slopes)
    return out.reshape(DB, DS, -1)


def _token_mixer(h, z_prev, wkv0, attn_fn, dims, w_in, rw_mu, rw_w0, rw_w2, rw_a0, rw_a2, rw_g2, rw_kk, rw_ka, rw_rk,
                 rw_gn_g, rw_gn_b, w_up_rw, w_up_at, w_out):
    B, T, _ = h.shape
    rw_in, at_in, at_heads, at_hd, idx_heads, idx_dim = dims
    at_w = at_heads * at_hd
    z = _matmul(h.reshape(B * T, -1), w_in, tm=IN_PROJ_TILE, tn=IN_PROJ_TILE).reshape(B, T, -1)
    z_rw, z_at, z_gate = jnp.split(z, [rw_in, rw_in + at_in], axis=-1)
    o_rw, wkv = _rwkv_branch(z_rw, z_prev, wkv0, rw_mu, rw_w0, rw_w2, rw_a0, rw_a2, rw_g2, rw_kk, rw_ka, rw_rk,
                             rw_gn_g, rw_gn_b)
    splits = [at_w, 2 * at_w, 3 * at_w, 3 * at_w + idx_heads * idx_dim, 3 * at_w + idx_heads * idx_dim + idx_dim]
    q, k, v, qi, ki, wi = jnp.split(z_at, splits, axis=-1)
    q = q.reshape(B, T, at_heads, at_hd)
    k = k.reshape(B, T, at_heads, at_hd)
    v = v.reshape(B, T, at_heads, at_hd)
    qi = qi.reshape(B, T, idx_heads, idx_dim)
    o_at = attn_fn(q, k, v, qi, ki, wi)
    g_rw, g_at = jnp.split(jax.nn.sigmoid(z_gate), 2, axis=-1)
    merged = g_rw * _mm3(o_rw, w_up_rw) + g_at * _mm3(o_at, w_up_at)
    return _mm3(merged, w_out), (k, v, ki, wkv, z_rw[:, -1])


def _top_values(s, k):
    n = s.shape[0]
    rows = lax.broadcasted_iota(jnp.int32, s.shape, 0)
    vals = []
    for _ in range(k):
        m = jnp.max(s, axis=0, keepdims=True)
        first = jnp.min(jnp.where(s == m, rows, n), axis=0, keepdims=True)
        s = jnp.where(rows == first, -jnp.inf, s)
        vals.append(m)
    return jnp.concatenate(vals, axis=0)


def _peer_route_body(q_ref, keys_ref, s1_ref, e1_ref, s2_ref, e2_ref, thr_ref, *, heads, half, topk):
    for h in range(heads):
        s = []
        for p in range(2):
            c0 = (2 * h + p) * half
            qhp = q_ref[:, c0:c0 + half].astype(MXU_DTYPE)
            s.append(lax.dot_general(keys_ref[h, p], qhp, (((1,), (1,)), ((), ())),
                                     preferred_element_type=jnp.float32))
        a = _top_values(s[0], topk)
        b = _top_values(s[1], topk)
        cand = jnp.concatenate([a[i:i + 1] + b for i in range(topk)], axis=0)
        cs = _top_values(cand, topk)
        z = jnp.sum(jnp.exp(cs - cs[0:1]), axis=0, keepdims=True)
        s1_ref[h] = s[0]
        s2_ref[h] = s[1]
        e1_ref[h] = jnp.exp(s[0] - a[0:1])
        e2_ref[h] = jnp.exp(s[1] - b[0:1]) / z
        thr_ref[h] = cs[topk - 1:topk]


def _gelu_exact(x):
    return 0.5 * x * (1.0 + lax.erf(x * (0.5 ** 0.5)))


def _peer_main_body(x_ref, u_ref, v_ref, s1_ref, e1_ref, s2_ref, e2_ref, thr_ref, o_ref, act_ref, wt_ref,
                    *, heads, nkeys, rows, lane_chunk):
    @pl.when(pl.program_id(1) == 0)
    def _():
        o_ref[...] = jnp.zeros_like(o_ref)

    t = x_ref.shape[0]
    act_ref[...] = lax.dot_general(u_ref[...], x_ref[...], (((1,), (1,)), ((), ())),
                                   preferred_element_type=jnp.float32)
    for il in range(rows):
        for c in range(t // lane_chunk):
            sl = slice(c * lane_chunk, (c + 1) * lane_chunk)
            g = jnp.zeros((nkeys, lane_chunk), jnp.float32)
            for h in range(heads):
                tot = s1_ref[h, 0, il:il + 1, sl] + s2_ref[h, :, sl]
                val = e1_ref[h, 0, il:il + 1, sl] * e2_ref[h, :, sl]
                g = g + jnp.where(tot >= thr_ref[h, :, sl], val, 0.0)
            w = g * _gelu_exact(act_ref[il * nkeys:(il + 1) * nkeys, sl])
            wt_ref[il * nkeys:(il + 1) * nkeys, sl] = w.astype(wt_ref.dtype)
    o_ref[...] += lax.dot_general(wt_ref[...], v_ref[...], (((0,), (0,)), ((), ())),
                                  preferred_element_type=jnp.float32)


def _peer_ffn(h, peer_wq, peer_keys, peer_u, peer_v):
    B, T, D = h.shape
    n = B * T
    heads, _, nkeys, half = peer_keys.shape
    experts = peer_u.shape[0]
    tok = min(PEER_TOKEN_TILE, n)
    rows = PEER_EXPERT_ROWS
    assert n % tok == 0 and nkeys % rows == 0 and experts == nkeys * nkeys
    x = h.reshape(n, D)
    q_all = _matmul(x, peer_wq)

    rt = min(PEER_ROUTE_TILE, n)
    tl_shape = jax.ShapeDtypeStruct((heads, nkeys, n), jnp.float32)
    tl_spec = pl.BlockSpec((heads, nkeys, rt), lambda i: (0, 0, i))
    s1, e1, s2, e2, thr = pl.pallas_call(
        functools.partial(_peer_route_body, heads=heads, half=half, topk=PEER_TOPK),
        grid=(n // rt,),
        in_specs=[
            pl.BlockSpec((rt, q_all.shape[1]), lambda i: (i, 0)),
            pl.BlockSpec(peer_keys.shape, lambda i: (0, 0, 0, 0)),
        ],
        out_specs=[tl_spec, tl_spec, tl_spec, tl_spec, pl.BlockSpec((heads, 1, rt), lambda i: (0, 0, i))],
        out_shape=[tl_shape, tl_shape, tl_shape, tl_shape, jax.ShapeDtypeStruct((heads, 1, n), jnp.float32)],
        compiler_params=pltpu.CompilerParams(dimension_semantics=("arbitrary",),
                                             vmem_limit_bytes=VMEM_LIMIT_BYTES),
        name="peer_route",
    )(q_all, peer_keys.astype(MXU_DTYPE))

    eb = rows * nkeys
    s1 = s1.reshape(heads, nkeys // rows, rows, n)
    e1 = e1.reshape(heads, nkeys // rows, rows, n)
    row_spec = pl.BlockSpec((heads, 1, rows, tok), lambda i, j: (0, j, 0, i))
    full_spec = pl.BlockSpec((heads, nkeys, tok), lambda i, j: (0, 0, i))
    out = pl.pallas_call(
        functools.partial(_peer_main_body, heads=heads, nkeys=nkeys, rows=rows,
                          lane_chunk=min(LANES, tok)),
        grid=(n // tok, experts // eb),
        in_specs=[
            pl.BlockSpec((tok, D), lambda i, j: (i, 0)),
            pl.BlockSpec((eb, D), lambda i, j: (j, 0)),
            pl.BlockSpec((eb, D), lambda i, j: (j, 0)),
            row_spec, row_spec, full_spec, full_spec,
            pl.BlockSpec((heads, 1, tok), lambda i, j: (0, 0, i)),
        ],
        out_specs=pl.BlockSpec((tok, D), lambda i, j: (i, 0)),
        out_shape=jax.ShapeDtypeStruct((n, D), jnp.float32),
        scratch_shapes=[pltpu.VMEM((eb, tok), jnp.float32), pltpu.VMEM((eb, tok), MXU_DTYPE)],
        compiler_params=pltpu.CompilerParams(dimension_semantics=("arbitrary", "arbitrary"),
                                             vmem_limit_bytes=VMEM_LIMIT_BYTES),
        name="peer_main",
    )(x.astype(MXU_DTYPE), peer_u.astype(MXU_DTYPE), peer_v.astype(MXU_DTYPE), s1, e1, s2, e2, thr)
    return out.reshape(B, T, D)


def kernel(x_prompt, x_sample, c_prompt, c_sample, cache_k, cache_v, cache_kidx, state_wkv, state_shift, page_table,
           w_ada, b_ada, w_in, rw_mu, rw_w0, rw_w2, rw_a0, rw_a2, rw_g2, rw_kk, rw_ka, rw_rk, rw_gn_g, rw_gn_b,
           w_up_rw, w_up_at, w_out, ln1_g, ln1_b, peer_wq, peer_keys, peer_u, peer_v, ln2_g, ln2_b):
    depth = w_in.shape[0]
    at_heads, at_hd = cache_k.shape[3], cache_k.shape[4]
    idx_dim = cache_kidx.shape[3]
    rw_in = rw_mu.shape[1]
    gate_in = 2 * x_prompt.shape[-1]
    at_in = w_in.shape[2] - rw_in - gate_in
    idx_heads = (at_in - 3 * at_heads * at_hd - idx_dim) // (idx_dim + 1)
    dims = (rw_in, at_in, at_heads, at_hd, idx_heads, idx_dim)
    alpha = (2 * depth) ** 0.25
    slopes = 2.0 ** (-8.0 * jnp.arange(1, at_heads + 1, dtype=jnp.float32) / at_heads)

    yp, ys = x_prompt, x_sample
    Bp = x_prompt.shape[0]
    H, N = rw_rk.shape[1:]
    zero_shift = jnp.zeros((Bp, rw_in), x_prompt.dtype)
    zero_wkv = jnp.zeros((Bp, H, N, N), x_prompt.dtype)
    new_p = [[], [], [], [], []]
    new_s = [[], [], [], [], []]
    for l in range(depth):
        mp, ms = _ada_mod(c_prompt, c_sample, w_ada[l], b_ada[l])
        mix_w = (w_in[l], rw_mu[l], rw_w0[l], rw_w2[l], rw_a0[l], rw_a2[l], rw_g2[l], rw_kk[l], rw_ka[l], rw_rk[l],
                 rw_gn_g[l], rw_gn_b[l], w_up_rw[l], w_up_at[l], w_out[l])
        prompt_attn = _prompt_sparse_attention
        sample_attn = functools.partial(_sample_sparse_attention, cache_k=cache_k[l], cache_v=cache_v[l],
                                        cache_kidx=cache_kidx[l], page_table=page_table, slopes=slopes)
        o_p, st_p = _token_mixer(yp * (1 + mp[1]) + mp[0], zero_shift, None, prompt_attn, dims, *mix_w)
        o_s, st_s = _token_mixer(ys * (1 + ms[1]) + ms[0], state_shift[l], state_wkv[l], sample_attn, dims, *mix_w)
        yp = _layer_norm(alpha * yp + mp[2] * o_p, ln1_g[l], ln1_b[l])
        ys = _layer_norm(alpha * ys + ms[2] * o_s, ln1_g[l], ln1_b[l])
        for i in range(5):
            new_p[i].append(st_p[i])
            new_s[i].append(st_s[i])
        peer_w = (peer_wq[l], peer_keys[l], peer_u[l], peer_v[l])
        yp = _layer_norm(alpha * yp + mp[5] * _peer_ffn(yp * (1 + mp[4]) + mp[3], *peer_w), ln2_g[l], ln2_b[l])
        ys = _layer_norm(alpha * ys + ms[5] * _peer_ffn(ys * (1 + ms[4]) + ms[3], *peer_w), ln2_g[l], ln2_b[l])
    k_p, v_p, kidx_p, wkv_p, shift_p = [jnp.stack(t) for t in new_p]
    k_s, v_s, kidx_s, wkv_s, shift_s = [jnp.stack(t) for t in new_s]
    return (yp, ys, k_p, v_p, kidx_p, wkv_p, shift_p, k_s, v_s, kidx_s, wkv_s, shift_s)
```

```python
import functools

import jax
import jax.numpy as jnp
from jax import lax
from jax.experimental import pallas as pl
from jax.experimental.pallas import tpu as pltpu

PAGE_SIZE = 128
TOPK_MAX = 256
Q_BLOCK = 128
PEER_TOPK = 16
TOKEN_BLOCK = 128
RW_GN_EPS = 64e-5
LN_EPS = 1e-5

MXU_DTYPE = jnp.bfloat16
LANES = 128
SUBLANES_BF16 = 16
VMEM_LIMIT_BYTES = 56 * 1024 * 1024

IN_PROJ_TILE = 1024
WKV_GROUP_WIDTH = 256
WKV_GROUPS_PER_STEP = 2
ATT_Q_TILE = 256
ATT_K_TILE = 512

PEER_ROUTE_TILE = 256
PEER_TOKEN_TILE = 512
PEER_EXPERT_ROWS = 4


def _round_up(n, m):
    return (n + m - 1) // m * m


def _matmul_body(a_ref, b_ref, o_ref):
    o_ref[...] = jnp.dot(a_ref[...], b_ref[...], preferred_element_type=jnp.float32)


def _matmul(a, b, tm=512, tn=512):
    m, k = a.shape
    _, n = b.shape
    tm, tn = min(tm, _round_up(m, SUBLANES_BF16)), min(tn, n)
    mp = _round_up(m, tm)
    a = a.astype(MXU_DTYPE)
    b = b.astype(MXU_DTYPE)
    if mp != m:
        a = jnp.pad(a, ((0, mp - m), (0, 0)))
    out = pl.pallas_call(
        _matmul_body,
        grid=(mp // tm, pl.cdiv(n, tn)),
        in_specs=[
            pl.BlockSpec((tm, k), lambda i, j: (i, 0)),
            pl.BlockSpec((k, tn), lambda i, j: (0, j)),
        ],
        out_specs=pl.BlockSpec((tm, tn), lambda i, j: (i, j)),
        out_shape=jax.ShapeDtypeStruct((mp, n), jnp.float32),
        compiler_params=pltpu.CompilerParams(
            dimension_semantics=("arbitrary", "arbitrary"),
            vmem_limit_bytes=VMEM_LIMIT_BYTES,
        ),
        name="matmul",
    )(a, b)
    return out if mp == m else out[:m]


def _mm3(x, w):
    b, t, d = x.shape
    return _matmul(x.reshape(b * t, d), w).reshape(b, t, -1)


def _layer_norm(x, g, b):
    mu = jnp.mean(x, -1, keepdims=True)
    var = jnp.mean(jnp.square(x - mu), -1, keepdims=True)
    return (x - mu) * lax.rsqrt(var + LN_EPS) * g + b


def _ada_mod(c_prompt, c_sample, w_ada, b_ada):
    c = jnp.concatenate([c_prompt, c_sample], axis=0)
    m = _matmul(jax.nn.silu(c), w_ada, tn=1024) + b_ada
    nb = c_prompt.shape[0]
    return jnp.split(m[:nb, None, :], 6, axis=-1), jnp.split(m[nb:, None, :], 6, axis=-1)


def _wkv7_scan(state, r, w, k, v, a, b):
    def step(S, inp):
        r_t, w_t, k_t, v_t, a_t, b_t = inp
        sa = jnp.einsum('bhvk,bhk->bhv', S, a_t)
        S = S * w_t[:, :, None, :] + sa[..., None] * b_t[:, :, None, :] + v_t[..., None] * k_t[:, :, None, :]
        return S, jnp.einsum('bhvk,bhk->bhv', S, r_t)
    xs = tuple(jnp.moveaxis(t, 1, 0) for t in (r, w, k, v, a, b))
    S, ys = lax.scan(step, state, xs)
    return S, jnp.moveaxis(ys, 0, 1)


_NN = (((1,), (0,)), ((), ()))
_NT = (((1,), (1,)), ((), ()))
_TN = (((0,), (0,)), ((), ()))


def _split_hi_lo(x):
    hi = x.astype(MXU_DTYPE)
    return hi, (x - hi.astype(jnp.float32)).astype(MXU_DTYPE)


def _dot1(a, b, dims):
    return lax.dot_general(a.astype(MXU_DTYPE), b.astype(MXU_DTYPE), dims, preferred_element_type=jnp.float32)


def _dot3(a, b, dims):
    ah, al = _split_hi_lo(a)
    bh, bl = _split_hi_lo(b)
    d = lambda x, y: lax.dot_general(x, y, dims, preferred_element_type=jnp.float32)
    return d(ah, bh) + (d(ah, bl) + d(al, bh))


def _wkv7_chunk_body(r_ref, ld_ref, k_ref, v_ref, a_ref, b_ref, y_ref, s_out_ref, s_ref, *, n, gw):
    c = r_ref.shape[0]
    gps = r_ref.shape[1] // gw
    hpg = gw // n

    @pl.when(pl.program_id(1) == 0)
    def _():
        s_ref[...] = jnp.zeros_like(s_ref)

    row = lax.broadcasted_iota(jnp.int32, (c, gw), 0)
    col = lax.broadcasted_iota(jnp.int32, (c, gw), 1)
    col_head, col_in = col // n, col % n
    eye_wide = jnp.where(col_in == row, 1.0, 0.0)
    strict, incl = col_in < row, col_in <= row
    sq_r = lax.broadcasted_iota(jnp.int32, (gw, gw), 0)
    sq_c = lax.broadcasted_iota(jnp.int32, (gw, gw), 1)
    same_head = (sq_r // n) == (sq_c // n)
    eye_sq = jnp.where(sq_r == sq_c, 1.0, 0.0)
    tril = jnp.where(lax.broadcasted_iota(jnp.int32, (c, c), 1) <= lax.broadcasted_iota(jnp.int32, (c, c), 0),
                     1.0, 0.0)

    def bd(x):
        return jnp.concatenate([jnp.where(col_head == h, x, 0.0) for h in range(hpg)], axis=0)

    for j in range(gps):
        sl = slice(j * gw, (j + 1) * gw)
        r, ld, k, v, a, b = (ref[:, sl] for ref in (r_ref, ld_ref, k_ref, v_ref, a_ref, b_ref))
        cs = _dot3(tril, ld, _NN)
        p_in = jnp.exp(cs)
        p_inv = jnp.exp(-cs)
        at, bt, kt, rt = a * jnp.exp(cs - ld), b * p_inv, k * p_inv, r * p_in
        bd_b, bd_k, bd_v = bd(bt), bd(kt), bd(v)
        lab = jnp.where(strict, _dot3(at, bd_b, _NT), 0.0)
        lak = jnp.where(strict, _dot3(at, bd_k, _NT), 0.0)
        mrb = jnp.where(incl, _dot3(rt, bd_b, _NT), 0.0)
        mrk = jnp.where(incl, _dot3(rt, bd_k, _NT), 0.0)

        pw = lab
        t = eye_wide + pw
        bd_pw = bd(pw)
        for _ in range(max(c - 1, 1).bit_length() - 1):
            pw = _dot1(pw, bd_pw, _NN)
            bd_pw = bd(pw)
            t = t + _dot1(t, bd_pw, _NN)

        w1 = _dot3(t, bd(at), _NN)
        x = _dot3(t, bd(_dot3(lak, bd_v, _NN)), _NN)
        s0 = s_ref[j]
        yw = rt + _dot3(mrb, bd(w1), _NN)
        y_ref[:, sl] = _dot3(yw, s0, _NT) + (_dot3(mrb, bd(x), _NN) + _dot3(mrk, bd_v, _NN))
        p_end = p_in[c - 1:c, :]
        g_mat = (eye_sq + jnp.where(same_head, _dot3(w1, bt, _TN), 0.0)) * p_end
        h_mat = jnp.where(same_head, _dot3(x, bt, _TN) + _dot3(v, kt, _TN), 0.0) * p_end
        s_new = _dot3(s0, g_mat, _NN) + h_mat
        s_ref[j] = s_new
        s_out_ref[j] = s_new


def _wkv7_chunked(r, log_decay, k, v, a, b, n):
    t_len, width = r.shape
    heads = width // n
    gw = min(WKV_GROUP_WIDTH, width)
    assert t_len % n == 0 and width % gw == 0 and gw % n == 0
    groups, hpg = width // gw, gw // n
    gps = WKV_GROUPS_PER_STEP if groups % WKV_GROUPS_PER_STEP == 0 else 1
    spec = pl.BlockSpec((n, gps * gw), lambda g, c: (c, g))
    y, s_bd = pl.pallas_call(
        functools.partial(_wkv7_chunk_body, n=n, gw=gw),
        grid=(groups // gps, t_len // n),
        in_specs=[spec] * 6,
        out_specs=[spec, pl.BlockSpec((gps, gw, gw), lambda g, c: (g, 0, 0))],
        out_shape=[jax.ShapeDtypeStruct((t_len, width), jnp.float32),
                   jax.ShapeDtypeStruct((groups, gw, gw), jnp.float32)],
        scratch_shapes=[pltpu.VMEM((gps, gw, gw), jnp.float32)],
        compiler_params=pltpu.CompilerParams(dimension_semantics=("arbitrary", "arbitrary"),
                                             vmem_limit_bytes=VMEM_LIMIT_BYTES),
        name="wkv7_chunked",
    )(r, log_decay, k, v, a, b)
    s5 = s_bd.reshape(groups, hpg, n, hpg, n)
    state = jnp.stack([s5[:, h, :, h, :] for h in range(hpg)], axis=1)
    return y, state.reshape(heads, n, n)


def _rwkv_branch(z_rw, z_prev, wkv0, rw_mu, rw_w0, rw_w2, rw_a0, rw_a2, rw_g2, rw_kk, rw_ka, rw_rk, rw_gn_g, rw_gn_b):
    B, T, _ = z_rw.shape
    H, N = rw_rk.shape
    W = H * N
    d_lora, a_lora = rw_w2.shape[0], rw_a2.shape[0]
    splits = [W, 2 * W, 3 * W, 3 * W + d_lora, 3 * W + d_lora + a_lora]
    z_shift = jnp.concatenate([z_prev[:, None, :], z_rw[:, :-1]], axis=1)
    zm = z_rw + (z_shift - z_rw) * rw_mu
    r, k, v, zw, za, zg = jnp.split(zm, splits, axis=-1)
    logw = -jax.nn.softplus(-(rw_w0 + jnp.tanh(zw) @ rw_w2)) - 0.5
    log_decay = -jnp.exp(logw)
    a = jax.nn.sigmoid(rw_a0 + za @ rw_a2)
    g = jax.nn.sigmoid(zg) @ rw_g2
    hd = lambda t: t.reshape(B, T, H, N)
    kk = hd(k * rw_kk)
    kk = kk / jnp.maximum(jnp.linalg.norm(kk, axis=-1, keepdims=True), 1e-12)
    k = k * (1.0 + (a - 1.0) * rw_ka)
    if wkv0 is None:
        assert B == 1
        y, state = _wkv7_chunked(r[0], log_decay[0], k[0], v[0], (-kk).reshape(T, W), (kk * hd(a)).reshape(T, W), N)
        y, state = y.reshape(B, T, H, N), state[None]
    else:
        state, y = _wkv7_scan(wkv0, hd(r), hd(jnp.exp(log_decay)), hd(k), hd(v), -kk, kk * hd(a))
    mu = jnp.mean(y, -1, keepdims=True)
    var = jnp.mean(jnp.square(y - mu), -1, keepdims=True)
    yn = ((y - mu) * lax.rsqrt(var + RW_GN_EPS)).reshape(B, T, W) * rw_gn_g + rw_gn_b
    bonus = jnp.sum(hd(r) * hd(k) * rw_rk, -1, keepdims=True) * hd(v)
    out = (yn + bonus.reshape(B, T, W)) * g
    return out, state


def _indexer_topk(q_idx, w_idx, k_idx, tq, topk):
    L = k_idx.shape[1]
    scale = (q_idx.shape[2] * q_idx.shape[3]) ** -0.5
    s = jax.nn.relu(jnp.einsum('bqhd,bld->bqhl', q_idx, k_idx))
    score = jnp.einsum('bqh,bqhl->bql', w_idx, s) * scale
    causal = jnp.arange(L)[None, None, :] <= tq[:, :, None]
    _, idx = lax.top_k(jnp.where(causal, score, -jnp.inf), topk)
    return idx, idx <= tq[..., None]


def _sparse_attend(q, k_sel, v_sel, idx, valid, tq, slopes):
    logits = jnp.einsum('bqhd,bqkhd->bqhk', q, k_sel) * q.shape[-1] ** -0.5
    dist = (tq[..., None] - idx).astype(jnp.float32)[:, :, None, :]
    logits = jnp.where(valid[:, :, None, :], logits - slopes[:, None] * dist, -jnp.inf)
    p = jax.nn.softmax(logits, axis=-1)
    return jnp.einsum('bqhk,bqkhd->bqhd', p, v_sel)


INT32_MIN = -2 ** 31
MASKED_LOGIT_FLOOR = -1e30


def _index_select_body(qi_ref, w_ref, ki_ref, bias_ref, keys_ref, *, idx_heads, idx_dim, tq, lk, topk, scale):
    t0 = pl.program_id(0) * tq
    nt = (t0 + tq + lk - 1) // lk
    n_tiles = bias_ref.shape[1]
    row = t0 + lax.broadcasted_iota(jnp.int32, (tq, lk), 0)
    lane = lax.broadcasted_iota(jnp.int32, (tq, lk), 1)

    def score_tile(kt, carry):
        k0 = pl.multiple_of(kt * lk, lk)
        k_tile = ki_ref[pl.ds(k0, lk), :]
        acc = jnp.zeros((tq, lk), jnp.float32)
        for h in range(idx_heads):
            s = lax.dot_general(qi_ref[:, h * idx_dim:(h + 1) * idx_dim], k_tile, (((1,), (1,)), ((), ())),
                                preferred_element_type=jnp.float32)
            acc = acc + w_ref[:, h:h + 1] * jnp.maximum(s, 0.0)
        bits = pltpu.bitcast(acc * scale, jnp.int32)
        key = jnp.where(bits < 0, bits ^ 0x7FFFFFFF, bits)
        keys_ref[kt] = jnp.where(lane + k0 <= row, key, INT32_MIN)
        return carry

    lax.fori_loop(0, nt, score_tile, 0)

    def bit_step(i, thr):
        cand = thr + lax.shift_left(jnp.int32(1), 31 - i)

        def count_tile(kt, c):
            hit = jnp.where(keys_ref[kt] >= cand, 1.0, 0.0)
            for j in range(lk // LANES):
                c = c + hit[:, j * LANES:(j + 1) * LANES]
            return c

        c = lax.fori_loop(0, nt, count_tile, jnp.zeros((tq, LANES), jnp.float32))
        return jnp.where(jnp.sum(c, axis=1, keepdims=True) >= topk, cand, thr)

    thr = lax.fori_loop(0, 32, bit_step, jnp.full((tq, 1), INT32_MIN, jnp.int32))
    thr = jnp.maximum(thr, INT32_MIN + 1)

    def bias_tile(kt, carry):
        bias_ref[0, kt] = jnp.where(keys_ref[kt] >= thr, 0.0, -jnp.inf).astype(bias_ref.dtype)
        return carry

    lax.fori_loop(0, nt, bias_tile, 0)

    def fill_tile(kt, carry):
        bias_ref[0, kt] = jnp.full((tq, lk), -jnp.inf, bias_ref.dtype)
        return carry

    lax.fori_loop(nt, n_tiles, fill_tile, 0)


def _lane_tile(x, reps):
    return x if reps == 1 else jnp.concatenate([x] * reps, axis=1)


def _masked_attn_body(q_ref, k_ref, v_ref, bias_ref, o_ref, m_ref, l_ref, acc_ref,
                      *, heads, hd, tq, lk, scale, slopes):
    qb, kt = pl.program_id(0), pl.program_id(1)
    last = ((qb + 1) * tq - 1) // lk

    @pl.when(kt == 0)
    def _():
        m_ref[...] = jnp.full(m_ref.shape, MASKED_LOGIT_FLOOR, jnp.float32)
        l_ref[...] = jnp.zeros_like(l_ref)
        acc_ref[...] = jnp.zeros_like(acc_ref)

    @pl.when(kt <= last)
    def _():
        bias = bias_ref[0, 0].astype(jnp.float32)
        rel = ((kt * lk + lax.broadcasted_iota(jnp.int32, (tq, lk), 1))
               - (qb * tq + lax.broadcasted_iota(jnp.int32, (tq, lk), 0))).astype(jnp.float32)
        for h in range(heads):
            cols = slice(h * hd, (h + 1) * hd)
            logits = lax.dot_general(q_ref[:, cols], k_ref[:, cols], (((1,), (1,)), ((), ())),
                                     preferred_element_type=jnp.float32) * scale
            logits = logits + (slopes[h] * rel + bias)
            m_old = m_ref[h]
            m_new = jnp.maximum(m_old, jnp.max(logits, axis=1, keepdims=True))
            alpha = jnp.exp(m_old - m_new)
            p = jnp.exp(logits - _lane_tile(m_new, lk // LANES))
            l_ref[h] = alpha * l_ref[h] + jnp.sum(p, axis=1, keepdims=True)
            m_ref[h] = m_new
            pv = jnp.dot(p.astype(v_ref.dtype), v_ref[:, cols], preferred_element_type=jnp.float32)
            acc_ref[:, cols] = _lane_tile(alpha, hd // LANES) * acc_ref[:, cols] + pv

    @pl.when(kt == pl.num_programs(1) - 1)
    def _():
        for h in range(heads):
            cols = slice(h * hd, (h + 1) * hd)
            o_ref[:, cols] = acc_ref[:, cols] / _lane_tile(l_ref[h], hd // LANES)


def _alibi_slopes(heads):
    return tuple(float(2.0 ** (-8.0 * (i + 1) / heads)) for i in range(heads))


def _prompt_sparse_attention(q, k, v, q_idx, k_idx, w_idx):
    B, S, heads, hd = q.shape
    idx_heads, idx_dim = q_idx.shape[2:]
    assert B == 1 and hd % LANES == 0
    topk = min(TOPK_MAX, S // 4)
    tq, lk = min(ATT_Q_TILE, S), min(ATT_K_TILE, S)
    assert S % tq == 0 and S % lk == 0 and lk % LANES == 0
    nq, nk = S // tq, S // lk

    bias = pl.pallas_call(
        functools.partial(_index_select_body, idx_heads=idx_heads, idx_dim=idx_dim, tq=tq, lk=lk, topk=topk,
                          scale=(idx_heads * idx_dim) ** -0.5),
        grid=(nq,),
        in_specs=[
            pl.BlockSpec((tq, idx_heads * idx_dim), lambda i: (i, 0)),
            pl.BlockSpec((tq, idx_heads), lambda i: (i, 0)),
            pl.BlockSpec((S, idx_dim), lambda i: (0, 0)),
        ],
        out_specs=pl.BlockSpec((1, nk, tq, lk), lambda i: (i, 0, 0, 0)),
        out_shape=jax.ShapeDtypeStruct((nq, nk, tq, lk), jnp.bfloat16),
        scratch_shapes=[pltpu.VMEM((nk, tq, lk), jnp.int32)],
        compiler_params=pltpu.CompilerParams(dimension_semantics=("arbitrary",),
                                             vmem_limit_bytes=VMEM_LIMIT_BYTES),
        name="index_select",
    )(q_idx.reshape(S, idx_heads * idx_dim).astype(MXU_DTYPE), w_idx.reshape(S, idx_heads),
      k_idx.reshape(S, idx_dim).astype(MXU_DTYPE))

    last = lambda i: ((i + 1) * tq - 1) // lk
    width = heads * hd
    out = pl.pallas_call(
        functools.partial(_masked_attn_body, heads=heads, hd=hd, tq=tq, lk=lk, scale=hd ** -0.5,
                          slopes=_alibi_slopes(heads)),
        grid=(nq, nk),
        in_specs=[
            pl.BlockSpec((tq, width), lambda i, j: (i, 0)),
            pl.BlockSpec((lk, width), lambda i, j: (jnp.minimum(j, last(i)), 0)),
            pl.BlockSpec((lk, width), lambda i, j: (jnp.minimum(j, last(i)), 0)),
            pl.BlockSpec((1, 1, tq, lk), lambda i, j: (i, jnp.minimum(j, last(i)), 0, 0)),
        ],
        out_specs=pl.BlockSpec((tq, width), lambda i, j: (i, 0)),
        out_shape=jax.ShapeDtypeStruct((S, width), jnp.float32),
        scratch_shapes=[pltpu.VMEM((heads, tq, LANES), jnp.float32), pltpu.VMEM((heads, tq, LANES), jnp.float32),
                        pltpu.VMEM((tq, width), jnp.float32)],
        compiler_params=pltpu.CompilerParams(dimension_semantics=("arbitrary", "arbitrary"),
                                             vmem_limit_bytes=VMEM_LIMIT_BYTES),
        name="masked_attention",
    )(q.reshape(S, width).astype(MXU_DTYPE), k.reshape(S, width).astype(MXU_DTYPE),
      v.reshape(S, width).astype(MXU_DTYPE), bias)
    return out.reshape(B, S, width)


def _sample_sparse_attention(q, k, v, q_idx, k_idx, w_idx, cache_k, cache_v, cache_kidx, page_table, slopes):
    DB, DS = q.shape[:2]
    n_pages = page_table.shape[1]
    past_len = n_pages * PAGE_SIZE
    L = past_len + DS
    topk = min(TOPK_MAX, L // 4)
    kidx_past = cache_kidx[page_table].reshape(DB, past_len, -1)
    kidx_all = jnp.concatenate([kidx_past, k_idx], axis=1)
    tq = past_len + jnp.broadcast_to(jnp.arange(DS), (DB, DS))
    idx, valid = _indexer_topk(q_idx, w_idx, kidx_all, tq, topk)
    in_past = idx < past_len
    pidx = jnp.minimum(idx, past_len - 1)
    bsel = jnp.arange(DB)[:, None, None]
    phys = page_table[bsel, pidx // PAGE_SIZE]
    slot = pidx % PAGE_SIZE
    nidx = jnp.clip(idx - past_len, 0, DS - 1)
    pick = lambda pool, new: jnp.where(in_past[..., None, None], pool[phys, slot], new[bsel, nidx])
    out = _sparse_attend(q, pick(cache_k, k), pick(cache_v, v), idx, valid, tq, slopes)
    return out.reshape(DB, DS, -1)


def _token_mixer(h, z_prev, wkv0, attn_fn, dims, w_in, rw_mu, rw_w0, rw_w2, rw_a0, rw_a2, rw_g2, rw_kk, rw_ka, rw_rk,
                 rw_gn_g, rw_gn_b, w_up_rw, w_up_at, w_out):
    B, T, _ = h.shape
    rw_in, at_in, at_heads, at_hd, idx_heads, idx_dim = dims
    at_w = at_heads * at_hd
    z = _matmul(h.reshape(B * T, -1), w_in, tm=IN_PROJ_TILE, tn=IN_PROJ_TILE).reshape(B, T, -1)
    z_rw, z_at, z_gate = jnp.split(z, [rw_in, rw_in + at_in], axis=-1)
    o_rw, wkv = _rwkv_branch(z_rw, z_prev, wkv0, rw_mu, rw_w0, rw_w2, rw_a0, rw_a2, rw_g2, rw_kk, rw_ka, rw_rk,
                             rw_gn_g, rw_gn_b)
    splits = [at_w, 2 * at_w, 3 * at_w, 3 * at_w + idx_heads * idx_dim, 3 * at_w + idx_heads * idx_dim + idx_dim]
    q, k, v, qi, ki, wi = jnp.split(z_at, splits, axis=-1)
    q = q.reshape(B, T, at_heads, at_hd)
    k = k.reshape(B, T, at_heads, at_hd)
    v = v.reshape(B, T, at_heads, at_hd)
    qi = qi.reshape(B, T, idx_heads, idx_dim)
    o_at = attn_fn(q, k, v, qi, ki, wi)
    g_rw, g_at = jnp.split(jax.nn.sigmoid(z_gate), 2, axis=-1)
    merged = g_rw * _mm3(o_rw, w_up_rw) + g_at * _mm3(o_at, w_up_at)
    return _mm3(merged, w_out), (k, v, ki, wkv, z_rw[:, -1])


def _top_values(s, k):
    n = s.shape[0]
    rows = lax.broadcasted_iota(jnp.int32, s.shape, 0)
    vals = []
    for _ in range(k):
        m = jnp.max(s, axis=0, keepdims=True)
        first = jnp.min(jnp.where(s == m, rows, n), axis=0, keepdims=True)
        s = jnp.where(rows == first, -jnp.inf, s)
        vals.append(m)
    return jnp.concatenate(vals, axis=0)


def _peer_route_body(q_ref, keys_ref, s1_ref, e1_ref, s2_ref, e2_ref, thr_ref, *, heads, half, topk):
    for h in range(heads):
        s = []
        for p in range(2):
            c0 = (2 * h + p) * half
            qhp = q_ref[:, c0:c0 + half].astype(MXU_DTYPE)
            s.append(lax.dot_general(keys_ref[h, p], qhp, (((1,), (1,)), ((), ())),
                                     preferred_element_type=jnp.float32))
        a = _top_values(s[0], topk)
        b = _top_values(s[1], topk)
        cand = jnp.concatenate([a[i:i + 1] + b for i in range(topk)], axis=0)
        cs = _top_values(cand, topk)
        z = jnp.sum(jnp.exp(cs - cs[0:1]), axis=0, keepdims=True)
        s1_ref[h] = s[0]
        s2_ref[h] = s[1]
        e1_ref[h] = jnp.exp(s[0] - a[0:1])
        e2_ref[h] = jnp.exp(s[1] - b[0:1]) / z
        thr_ref[h] = cs[topk - 1:topk]


def _gelu_exact(x):
    return 0.5 * x * (1.0 + lax.erf(x * (0.5 ** 0.5)))


def _peer_main_body(x_ref, u_ref, v_ref, s1_ref, e1_ref, s2_ref, e2_ref, thr_ref, o_ref, act_ref, wt_ref,
                    *, heads, nkeys, rows, lane_chunk):
    @pl.when(pl.program_id(1) == 0)
    def _():
        o_ref[...] = jnp.zeros_like(o_ref)

    t = x_ref.shape[0]
    act_ref[...] = lax.dot_general(u_ref[...], x_ref[...], (((1,), (1,)), ((), ())),
                                   preferred_element_type=jnp.float32)
    for il in range(rows):
        for c in range(t // lane_chunk):
            sl = slice(c * lane_chunk, (c + 1) * lane_chunk)
            g = jnp.zeros((nkeys, lane_chunk), jnp.float32)
            for h in range(heads):
                tot = s1_ref[h, 0, il:il + 1, sl] + s2_ref[h, :, sl]
                val = e1_ref[h, 0, il:il + 1, sl] * e2_ref[h, :, sl]
                g = g + jnp.where(tot >= thr_ref[h, :, sl], val, 0.0)
            w = g * _gelu_exact(act_ref[il * nkeys:(il + 1) * nkeys, sl])
            wt_ref[il * nkeys:(il + 1) * nkeys, sl] = w.astype(wt_ref.dtype)
    o_ref[...] += lax.dot_general(wt_ref[...], v_ref[...], (((0,), (0,)), ((), ())),
                                  preferred_element_type=jnp.float32)


def _peer_ffn(h, peer_wq, peer_keys, peer_u, peer_v):
    B, T, D = h.shape
    n = B * T
    heads, _, nkeys, half = peer_keys.shape
    experts = peer_u.shape[0]
    tok = min(PEER_TOKEN_TILE, n)
    rows = PEER_EXPERT_ROWS
    assert n % tok == 0 and nkeys % rows == 0 and experts == nkeys * nkeys
    x = h.reshape(n, D)
    q_all = _matmul(x, peer_wq)

    rt = min(PEER_ROUTE_TILE, n)
    tl_shape = jax.ShapeDtypeStruct((heads, nkeys, n), jnp.float32)
    tl_spec = pl.BlockSpec((heads, nkeys, rt), lambda i: (0, 0, i))
    s1, e1, s2, e2, thr = pl.pallas_call(
        functools.partial(_peer_route_body, heads=heads, half=half, topk=PEER_TOPK),
        grid=(n // rt,),
        in_specs=[
            pl.BlockSpec((rt, q_all.shape[1]), lambda i: (i, 0)),
            pl.BlockSpec(peer_keys.shape, lambda i: (0, 0, 0, 0)),
        ],
        out_specs=[tl_spec, tl_spec, tl_spec, tl_spec, pl.BlockSpec((heads, 1, rt), lambda i: (0, 0, i))],
        out_shape=[tl_shape, tl_shape, tl_shape, tl_shape, jax.ShapeDtypeStruct((heads, 1, n), jnp.float32)],
        compiler_params=pltpu.CompilerParams(dimension_semantics=("arbitrary",),
                                             vmem_limit_bytes=VMEM_LIMIT_BYTES),
        name="peer_route",
    )(q_all, peer_keys.astype(MXU_DTYPE))

    eb = rows * nkeys
    s1 = s1.reshape(heads, nkeys // rows, rows, n)
    e1 = e1.reshape(heads, nkeys // rows, rows, n)
    row_spec = pl.BlockSpec((heads, 1, rows, tok), lambda i, j: (0, j, 0, i))
    full_spec = pl.BlockSpec((heads, nkeys, tok), lambda i, j: (0, 0, i))
    out = pl.pallas_call(
        functools.partial(_peer_main_body, heads=heads, nkeys=nkeys, rows=rows,
                          lane_chunk=min(LANES, tok)),
        grid=(n // tok, experts // eb),
        in_specs=[
            pl.BlockSpec((tok, D), lambda i, j: (i, 0)),
            pl.BlockSpec((eb, D), lambda i, j: (j, 0)),
            pl.BlockSpec((eb, D), lambda i, j: (j, 0)),
            row_spec, row_spec, full_spec, full_spec,
            pl.BlockSpec((heads, 1, tok), lambda i, j: (0, 0, i)),
        ],
        out_specs=pl.BlockSpec((tok, D), lambda i, j: (i, 0)),
        out_shape=jax.ShapeDtypeStruct((n, D), jnp.float32),
        scratch_shapes=[pltpu.VMEM((eb, tok), jnp.float32), pltpu.VMEM((eb, tok), MXU_DTYPE)],
        compiler_params=pltpu.CompilerParams(dimension_semantics=("arbitrary", "arbitrary"),
                                             vmem_limit_bytes=VMEM_LIMIT_BYTES),
        name="peer_main",
    )(x.astype(MXU_DTYPE), peer_u.astype(MXU_DTYPE), peer_v.astype(MXU_DTYPE), s1, e1, s2, e2, thr)
    return out.reshape(B, T, D)


def kernel(x_prompt, x_sample, c_prompt, c_sample, cache_k, cache_v, cache_kidx, state_wkv, state_shift, page_table,
           w_ada, b_ada, w_in, rw_mu, rw_w0, rw_w2, rw_a0, rw_a2, rw_g2, rw_kk, rw_ka, rw_rk, rw_gn_g, rw_gn_b,
           w_up_rw, w_up_at, w_out, ln1_g, ln1_b, peer_wq, peer_keys, peer_u, peer_v, ln2_g, ln2_b):
    depth = w_in.shape[0]
    at_heads, at_hd = cache_k.shape[3], cache_k.shape[4]
    idx_dim = cache_kidx.shape[3]
    rw_in = rw_mu.shape[1]
    gate_in = 2 * x_prompt.shape[-1]
    at_in = w_in.shape[2] - rw_in - gate_in
    idx_heads = (at_in - 3 * at_heads * at_hd - idx_dim) // (idx_dim + 1)
    dims = (rw_in, at_in, at_heads, at_hd, idx_heads, idx_dim)
    alpha = (2 * depth) ** 0.25
    slopes = 2.0 ** (-8.0 * jnp.arange(1, at_heads + 1, dtype=jnp.float32) / at_heads)

    yp, ys = x_prompt, x_sample
    Bp = x_prompt.shape[0]
    H, N = rw_rk.shape[1:]
    zero_shift = jnp.zeros((Bp, rw_in), x_prompt.dtype)
    zero_wkv = jnp.zeros((Bp, H, N, N), x_prompt.dtype)
    new_p = [[], [], [], [], []]
    new_s = [[], [], [], [], []]
    for l in range(depth):
        mp, ms = _ada_mod(c_prompt, c_sample, w_ada[l], b_ada[l])
        mix_w = (w_in[l], rw_mu[l], rw_w0[l], rw_w2[l], rw_a0[l], rw_a2[l], rw_g2[l], rw_kk[l], rw_ka[l], rw_rk[l],
                 rw_gn_g[l], rw_gn_b[l], w_up_rw[l], w_up_at[l], w_out[l])
        prompt_attn = _prompt_sparse_attention
        sample_attn = functools.partial(_sample_sparse_attention, cache_k=cache_k[l], cache_v=cache_v[l],
                                        cache_kidx=cache_kidx[l], page_table=page_table, slopes=slopes)
        o_p, st_p = _token_mixer(yp * (1 + mp[1]) + mp[0], zero_shift, None, prompt_attn, dims, *mix_w)
        o_s, st_s = _token_mixer(ys * (1 + ms[1]) + ms[0], state_shift[l], state_wkv[l], sample_attn, dims, *mix_w)
        yp = _layer_norm(alpha * yp + mp[2] * o_p, ln1_g[l], ln1_b[l])
        ys = _layer_norm(alpha * ys + ms[2] * o_s, ln1_g[l], ln1_b[l])
        for i in range(5):
            new_p[i].append(st_p[i])
            new_s[i].append(st_s[i])
        peer_w = (peer_wq[l], peer_keys[l], peer_u[l], peer_v[l])
        yp = _layer_norm(alpha * yp + mp[5] * _peer_ffn(yp * (1 + mp[4]) + mp[3], *peer_w), ln2_g[l], ln2_b[l])
        ys = _layer_norm(alpha * ys + ms[5] * _peer_ffn(ys * (1 + ms[4]) + ms[3], *peer_w), ln2_g[l], ln2_b[l])
    k_p, v_p, kidx_p, wkv_p, shift_p = [jnp.stack(t) for t in new_p]
    k_s, v_s, kidx_s, wkv_s, shift_s = [jnp.stack(t) for t in new_s]
    return (yp, ys, k_p, v_p, kidx_p, wkv_p, shift_p, k_s, v_s, kidx_s, wkv_s, shift_s)
```
